```python
import jax, jax.numpy as jnp
from jax import lax
import numpy as np

D_MODEL = 2048
BATCH = 16
SEQ = 2048
DEPTH = 1
DEC_BATCH = 16
DEC_SEQ = 16
PAST_LEN = 1024

CHUNK = 64
N_HEADS = 8
HEAD_DIM = 128
ATT_W = N_HEADS * HEAD_DIM
D_CONV = D_MODEL // 2
CONV_W = 31
FFN_CONV_W = 3
D_FF = 5632
Q_BLOCK = 128
EPS = 1e-6
NEG_INF = -1e30
N_IN = 2 * D_CONV + 3 * ATT_W + N_HEADS + 2 * D_MODEL

kernel_name = "hybrid_conformer_fox_streaming_step"


def rmsnorm(x, g):
    xf = x.astype(jnp.float32)
    r = xf * lax.rsqrt(jnp.mean(xf * xf, axis=-1, keepdims=True) + EPS)
    return (r * g.astype(jnp.float32)).astype(x.dtype)


def layernorm(x, g, b):
    xf = x.astype(jnp.float32)
    mu = jnp.mean(xf, axis=-1, keepdims=True)
    xc = xf - mu
    r = xc * lax.rsqrt(jnp.mean(xc * xc, axis=-1, keepdims=True) + EPS)
    return (r * g.astype(jnp.float32) + b.astype(jnp.float32)).astype(x.dtype)


def causal_dwconv(x_ext, w, b):
    c = x_ext.shape[-1]
    y = lax.conv_general_dilated(
        x_ext, w[:, None, :].astype(x_ext.dtype), window_strides=(1,), padding="VALID",
        dimension_numbers=("NWC", "WIO", "NWC"), feature_group_count=c)
    return y + b.astype(y.dtype)


def fox_probs(s, bias, mask):
    return jax.nn.softmax(jnp.where(mask[None, None], s + bias, NEG_INF), axis=-1)


def fox_attention_prompt(q, k, v, logf):
    b, s, h, d = q.shape
    nb = s // Q_BLOCK
    scale = d ** -0.5
    c_t = jnp.transpose(jnp.cumsum(logf, axis=1), (0, 2, 1))
    qb = jnp.transpose(q.reshape(b, nb, Q_BLOCK, h, d), (1, 0, 2, 3, 4))
    cb = jnp.transpose(c_t.reshape(b, h, nb, Q_BLOCK), (2, 0, 1, 3))
    key_pos = jnp.arange(s)

    def block(args):
        i, qi, ci = args
        sc = jnp.einsum("bqhd,bkhd->bhqk", qi, k, preferred_element_type=jnp.float32) * scale
        bias = ci[..., :, None] - c_t[:, :, None, :]
        q_pos = i * Q_BLOCK + jnp.arange(Q_BLOCK)
        p = fox_probs(sc, bias, key_pos[None, :] <= q_pos[:, None])
        return jnp.einsum("bhqk,bkhd->bqhd", p.astype(v.dtype), v)

    o = lax.map(block, (jnp.arange(nb, dtype=jnp.int32), qb, cb))
    return jnp.transpose(o, (1, 0, 2, 3, 4)).reshape(b, s, h * d)


def fox_attention_sample(q, k, v, logf, past_k, past_v, past_logf):
    b, t, h, d = q.shape
    p_len = past_k.shape[1]
    scale = d ** -0.5
    kk = jnp.concatenate([past_k.astype(k.dtype), k], axis=1)
    vv = jnp.concatenate([past_v.astype(v.dtype), v], axis=1)
    lf = jnp.concatenate([past_logf.astype(jnp.float32), logf], axis=1)
    c_t = jnp.transpose(jnp.cumsum(lf, axis=1), (0, 2, 1))
    sc = jnp.einsum("bqhd,bkhd->bhqk", q, kk, preferred_element_type=jnp.float32) * scale
    bias = c_t[:, :, p_len:, None] - c_t[:, :, None, :]
    mask = jnp.arange(p_len + t)[None, :] <= (p_len + jnp.arange(t))[:, None]
    p = fox_probs(sc, bias, mask)
    o = jnp.einsum("bhqk,bkhd->bqhd", p.astype(vv.dtype), vv)
    return o.reshape(b, t, h * d)


def hybrid_layer(x, conv_buf, ffn_buf, attn_fn,
                 norm_mix_g, w_in, b_forget, conv_dw_w, conv_dw_b, conv_ln_g, conv_ln_b,
                 conv_pw_out, q_norm_g, k_norm_g, w_att_out, w_out,
                 norm_ffn_g, w_up, ffn_dw_w, ffn_dw_b, w_down):
    b, t, _ = x.shape
    h = rmsnorm(x, norm_mix_g)
    proj = h @ w_in
    o0 = 2 * D_CONV
    o1 = o0 + ATT_W
    o2 = o1 + ATT_W
    o3 = o2 + ATT_W
    o4 = o3 + N_HEADS
    o5 = o4 + D_MODEL
    glu_in = proj[..., :o0]
    q = proj[..., o0:o1].reshape(b, t, N_HEADS, HEAD_DIM)
    k = proj[..., o1:o2].reshape(b, t, N_HEADS, HEAD_DIM)
    v = proj[..., o2:o3].reshape(b, t, N_HEADS, HEAD_DIM)
    f_logit = proj[..., o3:o4]
    gate_a = jax.nn.sigmoid(proj[..., o4:o5])
    gate_b = jax.nn.sigmoid(proj[..., o5:])

    u = glu_in[..., :D_CONV] * jax.nn.sigmoid(glu_in[..., D_CONV:])
    ext = jnp.concatenate([conv_buf.astype(u.dtype), u], axis=1)
    z = layernorm(causal_dwconv(ext, conv_dw_w, conv_dw_b), conv_ln_g, conv_ln_b)
    a_out = (z * jax.nn.sigmoid(z)) @ conv_pw_out
    new_conv = ext[:, -(CONV_W - 1):]

    q = rmsnorm(q, q_norm_g)
    k = rmsnorm(k, k_norm_g)
    logf = jax.nn.log_sigmoid(f_logit.astype(jnp.float32) + b_forget.astype(jnp.float32))
    b_out = attn_fn(q, k, v, logf) @ w_att_out

    x = x + (gate_a * a_out + gate_b * b_out) @ w_out

    h2 = rmsnorm(x, norm_ffn_g)
    up = h2 @ w_up
    g_half = up[..., :D_FF]
    u_half = up[..., D_FF:]
    ext_f = jnp.concatenate([ffn_buf.astype(g_half.dtype), g_half], axis=1)
    gc = causal_dwconv(ext_f, ffn_dw_w, ffn_dw_b)
    x = x + (jax.nn.silu(gc) * u_half) @ w_down
    new_ffn = ext_f[:, -(FFN_CONV_W - 1):]
    return x, new_conv, new_ffn, k, v, logf


def setup_inputs(seed: int = 0) -> dict:
    key = jax.random.key(seed)
    ks = jax.random.split(key, 32)
    f32 = jnp.float32
    nrm = lambda k, shape, s: (jax.random.normal(k, shape, f32) * s)
    b_forget = (jnp.linspace(1.0, 6.0, N_HEADS, dtype=f32)[None, :]
                + nrm(ks[7], (DEPTH, N_HEADS), 0.1))
    return {
        "x_prompt": nrm(ks[0], (BATCH, SEQ, D_MODEL), 1.0),
        "x_sample": nrm(ks[1], (DEC_BATCH, DEC_SEQ, D_MODEL), 1.0),
        "cache_k": nrm(ks[2], (DEPTH, DEC_BATCH, PAST_LEN, N_HEADS, HEAD_DIM), 1.0),
        "cache_v": nrm(ks[3], (DEPTH, DEC_BATCH, PAST_LEN, N_HEADS, HEAD_DIM), 1.0),
        "cache_logf": jax.nn.log_sigmoid(b_forget[:, None, None, :]
                                          + nrm(ks[4], (DEPTH, DEC_BATCH, PAST_LEN, N_HEADS), 1.0)),
        "state_conv": nrm(ks[5], (DEPTH, DEC_BATCH, CONV_W - 1, D_CONV), 0.5),
        "state_ffn": nrm(ks[6], (DEPTH, DEC_BATCH, FFN_CONV_W - 1, D_FF), 1.0),
        "norm_mix_g": 1.0 + nrm(ks[8], (DEPTH, D_MODEL), 0.05),
        "w_in": nrm(ks[9], (DEPTH, D_MODEL, N_IN), D_MODEL ** -0.5),
        "b_forget": b_forget,
        "conv_dw_w": nrm(ks[10], (DEPTH, CONV_W, D_CONV), CONV_W ** -0.5),
        "conv_dw_b": nrm(ks[11], (DEPTH, D_CONV), 0.02),
        "conv_ln_g": 1.0 + nrm(ks[12], (DEPTH, D_CONV), 0.05),
        "conv_ln_b": nrm(ks[13], (DEPTH, D_CONV), 0.02),
        "conv_pw_out": nrm(ks[14], (DEPTH, D_CONV, D_MODEL), D_CONV ** -0.5),
        "q_norm_g": 1.0 + nrm(ks[15], (DEPTH, HEAD_DIM), 0.05),
        "k_norm_g": 1.0 + nrm(ks[16], (DEPTH, HEAD_DIM), 0.05),
        "w_att_out": nrm(ks[17], (DEPTH, ATT_W, D_MODEL), ATT_W ** -0.5),
        "w_out": nrm(ks[18], (DEPTH, D_MODEL, D_MODEL), D_MODEL ** -0.5),
        "norm_ffn_g": 1.0 + nrm(ks[19], (DEPTH, D_MODEL), 0.05),
        "w_up": nrm(ks[20], (DEPTH, D_MODEL, 2 * D_FF), D_MODEL ** -0.5),
        "ffn_dw_w": nrm(ks[21], (DEPTH, FFN_CONV_W, D_FF), FFN_CONV_W ** -0.5),
        "ffn_dw_b": nrm(ks[22], (DEPTH, D_FF), 0.02),
        "w_down": nrm(ks[23], (DEPTH, D_FF, D_MODEL), D_FF ** -0.5),
    }


def reference(x_prompt, x_sample, cache_k, cache_v, cache_logf, state_conv, state_ffn,
              norm_mix_g, w_in, b_forget, conv_dw_w, conv_dw_b, conv_ln_g, conv_ln_b,
              conv_pw_out, q_norm_g, k_norm_g, w_att_out, w_out,
              norm_ffn_g, w_up, ffn_dw_w, ffn_dw_b, w_down):
    bp = x_prompt.shape[0]
    xp = x_prompt
    xs = x_sample
    pk, pv, plf, pconv, pffn = [], [], [], [], []
    sk, sv, slf, sconv, sffn = [], [], [], [], []
    for l in range(DEPTH):
        weights = (norm_mix_g[l], w_in[l], b_forget[l], conv_dw_w[l], conv_dw_b[l],
                   conv_ln_g[l], conv_ln_b[l], conv_pw_out[l], q_norm_g[l], k_norm_g[l],
                   w_att_out[l], w_out[l], norm_ffn_g[l], w_up[l], ffn_dw_w[l], ffn_dw_b[l],
                   w_down[l])
        zero_conv = jnp.zeros((bp, CONV_W - 1, D_CONV), xp.dtype)
        zero_ffn = jnp.zeros((bp, FFN_CONV_W - 1, D_FF), xp.dtype)
        xp, c_p, f_p, k_p, v_p, lf_p = hybrid_layer(
            xp, zero_conv, zero_ffn, fox_attention_prompt, *weights)
        ck, cv, clf = cache_k[l], cache_v[l], cache_logf[l]
        attn_s = lambda q, k, v, lf, ck=ck, cv=cv, clf=clf: fox_attention_sample(q, k, v, lf, ck, cv, clf)
        xs, c_s, f_s, k_s, v_s, lf_s = hybrid_layer(
            xs, state_conv[l], state_ffn[l], attn_s, *weights)
        pk.append(k_p); pv.append(v_p); plf.append(lf_p); pconv.append(c_p); pffn.append(f_p)
        sk.append(k_s); sv.append(v_s); slf.append(lf_s); sconv.append(c_s); sffn.append(f_s)
    return (xp, xs,
            jnp.stack(pk), jnp.stack(pv), jnp.stack(plf), jnp.stack(pconv), jnp.stack(pffn),
            jnp.stack(sk), jnp.stack(sv), jnp.stack(slf), jnp.stack(sconv), jnp.stack(sffn))
```

```python
import functools

import jax
import jax.numpy as jnp
from jax import lax
from jax.experimental import pallas as pl
from jax.experimental.pallas import tpu as pltpu

F32 = jnp.float32
BF16 = jnp.bfloat16
EPS = 1e-6
NEG_INF = -1e30
LANES = 128
SUBLANES = 8
VMEM_LIMIT_BYTES = 60 * 1024 * 1024
N_TILE = 512
GLU_HALF = N_TILE // 2


def _params(n_axes, vmem=VMEM_LIMIT_BYTES):
    return pltpu.CompilerParams(dimension_semantics=("arbitrary",) * n_axes,
                                vmem_limit_bytes=vmem)


def _const_spec(shape):
    nd = len(shape)
    return pl.BlockSpec(shape, lambda *_: (0,) * nd, pipeline_mode=pl.Buffered(1))


def _sigmoid(x):
    return jax.nn.sigmoid(x)


def _in_proj_kernel(x_ref, g_ref, w_ref, wf_ref, bf_ref, qg_ref, kg_ref,
                    u_ref, q_ref, kf_ref, kb_ref, vf_ref, vb_ref, ga_ref, gb_ref,
                    lf_ref, lft_ref, h_ref, *, rc, n_glu, n_att, n_gate, n_heads, scale):
    j = pl.program_id(1)
    tm = x_ref.shape[0]
    n_chunks = tm // rc
    j_q = n_glu
    j_k = j_q + n_att
    j_v = j_k + n_att
    j_ga = j_v + n_att
    j_gb = j_ga + n_gate
    heads_per_tile = N_TILE // LANES

    @pl.when(j == 0)
    def _():
        for c in range(n_chunks):
            rows = pl.ds(c * rc, rc)
            x = x_ref[rows, :]
            ms = jnp.mean(x * x, axis=-1, keepdims=True)
            hb = ((x * lax.rsqrt(ms + EPS)) * g_ref[...]).astype(BF16)
            h_ref[rows, :] = hb
            f = jnp.dot(hb, wf_ref[...], preferred_element_type=F32) + bf_ref[...]
            lf = jnp.minimum(f, 0.0) - jnp.log1p(jnp.exp(-jnp.abs(f)))
            lf_ref[rows, :] = lf[:, :n_heads]
            lft_ref[0, :, rows] = lf.T[:n_heads, :]

    def chunks(epilogue):
        for c in range(n_chunks):
            rows = pl.ds(c * rc, rc)
            acc = jnp.dot(h_ref[rows, :], w_ref[...], preferred_element_type=F32)
            epilogue(rows, acc)

    def head_norm(acc, gain_ref, hh):
        xh = acc[:, hh * LANES:(hh + 1) * LANES]
        ms = jnp.mean(xh * xh, axis=-1, keepdims=True)
        return (xh * lax.rsqrt(ms + EPS)) * gain_ref[...]

    @pl.when(j < j_q)
    def _():
        def ep(rows, acc):
            u_ref[rows, :] = acc[:, :GLU_HALF] * _sigmoid(acc[:, GLU_HALF:])
        chunks(ep)

    @pl.when((j >= j_q) & (j < j_k))
    def _():
        def ep(rows, acc):
            for hh in range(heads_per_tile):
                qn = head_norm(acc, qg_ref, hh) * scale
                q_ref[rows, hh * LANES:(hh + 1) * LANES] = qn.astype(BF16)
        chunks(ep)

    @pl.when((j >= j_k) & (j < j_v))
    def _():
        def ep(rows, acc):
            for hh in range(heads_per_tile):
                kn = head_norm(acc, kg_ref, hh)
                kf_ref[rows, hh * LANES:(hh + 1) * LANES] = kn
                kb_ref[rows, hh * LANES:(hh + 1) * LANES] = kn.astype(BF16)
        chunks(ep)

    @pl.when((j >= j_v) & (j < j_ga))
    def _():
        def ep(rows, acc):
            vf_ref[rows, :] = acc
            vb_ref[rows, :] = acc.astype(BF16)
        chunks(ep)

    @pl.when((j >= j_ga) & (j < j_gb))
    def _():
        def ep(rows, acc):
            ga_ref[rows, :] = _sigmoid(acc).astype(BF16)
        chunks(ep)

    @pl.when(j >= j_gb)
    def _():
        def ep(rows, acc):
            gb_ref[rows, :] = _sigmoid(acc).astype(BF16)
        chunks(ep)


def _in_proj(x2d, norm_g, w_tiles, wf_pad, bf_pad, q_gain, k_gain, *, n_seq, d_conv, att_w,
             n_heads, tm, rc):
    m, d_model = x2d.shape
    seq = m // n_seq
    tiles_per_seq = seq // tm
    n_glu = d_conv // GLU_HALF
    n_att = att_w // N_TILE
    n_gate = d_model // N_TILE
    n_tiles = n_glu + 3 * n_att + 2 * n_gate
    scale = float(LANES) ** -0.5
    j_q, j_k, j_v = n_glu, n_glu + n_att, n_glu + 2 * n_att
    j_ga = n_glu + 3 * n_att
    j_gb = j_ga + n_gate

    def col(start, count):
        return lambda i, j: (i, jnp.clip(j - start, 0, count - 1))

    kernel = functools.partial(_in_proj_kernel, rc=rc, n_glu=n_glu, n_att=n_att,
                               n_gate=n_gate, n_heads=n_heads, scale=scale)
    out_shape = (
        jax.ShapeDtypeStruct((m, d_conv), F32),
        jax.ShapeDtypeStruct((m, att_w), BF16),
        jax.ShapeDtypeStruct((m, att_w), F32),
        jax.ShapeDtypeStruct((m, att_w), BF16),
        jax.ShapeDtypeStruct((m, att_w), F32),
        jax.ShapeDtypeStruct((m, att_w), BF16),
        jax.ShapeDtypeStruct((m, d_model), BF16),
        jax.ShapeDtypeStruct((m, d_model), BF16),
        jax.ShapeDtypeStruct((m, n_heads), F32),
        jax.ShapeDtypeStruct((n_seq, n_heads, seq), F32),
    )
    out_specs = (
        pl.BlockSpec((tm, GLU_HALF), col(0, n_glu)),
        pl.BlockSpec((tm, N_TILE), col(j_q, n_att)),
        pl.BlockSpec((tm, N_TILE), col(j_k, n_att)),
        pl.BlockSpec((tm, N_TILE), col(j_k, n_att)),
        pl.BlockSpec((tm, N_TILE), col(j_v, n_att)),
        pl.BlockSpec((tm, N_TILE), col(j_v, n_att)),
        pl.BlockSpec((tm, N_TILE), col(j_ga, n_gate)),
        pl.BlockSpec((tm, N_TILE), col(j_gb, n_gate)),
        pl.BlockSpec((tm, n_heads), lambda i, j: (i, 0)),
        pl.BlockSpec((1, n_heads, tm), lambda i, j: (i // tiles_per_seq, 0, i % tiles_per_seq)),
    )
    in_specs = [
        pl.BlockSpec((tm, d_model), lambda i, j: (i, 0)),
        _const_spec((1, d_model)),
        pl.BlockSpec((d_model, N_TILE), lambda i, j: (0, j)),
        _const_spec((d_model, LANES)),
        _const_spec((1, LANES)),
        _const_spec((1, LANES)),
        _const_spec((1, LANES)),
    ]
    return pl.pallas_call(
        kernel,
        grid=(m // tm, n_tiles),
        in_specs=in_specs,
        out_specs=out_specs,
        out_shape=out_shape,
        scratch_shapes=[pltpu.VMEM((tm, d_model), BF16)],
        compiler_params=_params(2),
        name="in_proj",
    )(x2d, norm_g, w_tiles, wf_pad, bf_pad, q_gain, k_gain)


def _cumsum_kernel(lf_ref, tri_ref, c_ref):
    n_blocks = lf_ref.shape[2] // LANES
    run = jnp.zeros((lf_ref.shape[1], 1), F32)
    for blk in range(n_blocks):
        cols = slice(blk * LANES, (blk + 1) * LANES)
        cb = jnp.dot(lf_ref[0, :, cols], tri_ref[...], precision=lax.Precision.HIGHEST,
                     preferred_element_type=F32) + run
        c_ref[0, :, cols] = cb
        run = cb[:, LANES - 1:LANES]


def _cumsum(lft):
    n_seq, n_heads, length = lft.shape
    idx = jnp.arange(LANES)
    tri = (idx[:, None] <= idx[None, :]).astype(F32)
    return pl.pallas_call(
        _cumsum_kernel,
        grid=(n_seq,),
        in_specs=[pl.BlockSpec((1, n_heads, length), lambda b: (b, 0, 0)),
                  _const_spec((LANES, LANES))],
        out_specs=pl.BlockSpec((1, n_heads, length), lambda b: (b, 0, 0)),
        out_shape=jax.ShapeDtypeStruct(lft.shape, F32),
        compiler_params=_params(1),
        name="logf_cumsum",
    )(lft, tri)


def _conv_kernel(u_ref, ctx_ref, w_ref, b_ref, lg_ref, lb_ref, z_ref, xbuf, ybuf,
                 *, n_taps, row_step, halo, rb_rows):
    i = pl.program_id(1)
    tc, d_conv = u_ref.shape
    first_tap = halo - (n_taps - 1) * row_step

    @pl.when(i == 0)
    def _():
        xbuf[0:halo, :] = ctx_ref[0]

    @pl.when(i > 0)
    def _():
        xbuf[0:halo, :] = xbuf[tc:tc + halo, :]

    xbuf[halo:halo + tc, :] = u_ref[...]

    def lane_chunk(cc, carry):
        cols = pl.ds(pl.multiple_of(cc * LANES, LANES), LANES)
        for rb in range(tc // rb_rows):
            r0 = rb * rb_rows
            acc = jnp.broadcast_to(b_ref[:, cols], (rb_rows, LANES))
            for k in range(n_taps):
                xk = xbuf[pl.ds(r0 + first_tap + k * row_step, rb_rows), cols]
                acc = acc + w_ref[k:k + 1, cols] * xk
            ybuf[pl.ds(r0, rb_rows), cols] = acc
        return carry

    lax.fori_loop(0, d_conv // LANES, lane_chunk, 0)

    for rb in range(tc // rb_rows):
        rows = pl.ds(rb * rb_rows, rb_rows)
        y = ybuf[rows, :]
        mu = jnp.mean(y, axis=-1, keepdims=True)
        yc = y - mu
        var = jnp.mean(yc * yc, axis=-1, keepdims=True)
        zn = (yc * lax.rsqrt(var + EPS)) * lg_ref[...] + lb_ref[...]
        z_ref[rows, :] = (zn * _sigmoid(zn)).astype(BF16)


def _conv(u2d, ctx, w_pad, bias, ln_g, ln_b, *, n_seq, n_taps, row_step, tc, rb_rows):
    m, d_conv = u2d.shape
    halo = ctx.shape[1]
    tiles_per_seq = (m // n_seq) // tc
    kernel = functools.partial(_conv_kernel, n_taps=n_taps, row_step=row_step, halo=halo,
                               rb_rows=rb_rows)
    return pl.pallas_call(
        kernel,
        grid=(n_seq, tiles_per_seq),
        in_specs=[
            pl.BlockSpec((tc, d_conv), lambda b, i: (b * tiles_per_seq + i, 0)),
            pl.BlockSpec((1, halo, d_conv), lambda b, i: (b, 0, 0)),
            _const_spec(w_pad.shape),
            _const_spec((1, d_conv)),
            _const_spec((1, d_conv)),
            _const_spec((1, d_conv)),
        ],
        out_specs=pl.BlockSpec((tc, d_conv), lambda b, i: (b * tiles_per_seq + i, 0)),
        out_shape=jax.ShapeDtypeStruct((m, d_conv), BF16),
        scratch_shapes=[pltpu.VMEM((halo + tc, d_conv), F32), pltpu.VMEM((tc, d_conv), F32)],
        compiler_params=_params(2),
        name="conv_branch",
    )(u2d, ctx, w_pad, bias, ln_g, ln_b)


def _attn_kernel(q_ref, k_ref, v_ref, c_ref, o_ref, m_sc, l_sc, acc_sc, *, tq):
    h = pl.program_id(1)
    i = pl.program_id(2)
    q = q_ref[...]
    m_sc[...] = jnp.full(m_sc.shape, NEG_INF, F32)
    l_sc[...] = jnp.zeros(l_sc.shape, F32)
    acc_sc[...] = jnp.zeros(acc_sc.shape, F32)

    def step(j, masked):
        ks = pl.multiple_of(j * tq, tq)
        kj = k_ref[0, pl.ds(ks, tq), :]
        vj = v_ref[0, pl.ds(ks, tq), :]
        cj = c_ref[0, pl.ds(h, 1), pl.ds(ks, tq)]
        s = lax.dot_general(q, kj, (((1,), (1,)), ((), ())), preferred_element_type=F32) - cj
        if masked:
            row = lax.broadcasted_iota(jnp.int32, (tq, tq), 0)
            colk = lax.broadcasted_iota(jnp.int32, (tq, tq), 1)
            s = jnp.where(colk <= row, s, NEG_INF)
        m_old = m_sc[...]
        m_new = jnp.maximum(m_old, jnp.max(s, axis=1, keepdims=True))
        alpha = jnp.exp(m_old - m_new)
        p = jnp.exp(s - m_new)
        l_sc[...] = alpha * l_sc[...] + jnp.sum(p, axis=1, keepdims=True)
        acc_sc[...] = alpha * acc_sc[...] + jnp.dot(p.astype(BF16), vj,
                                                    preferred_element_type=F32)
        m_sc[...] = m_new

    def body(j, carry):
        step(j, False)
        return carry

    lax.fori_loop(0, i, body, 0)
    step(i, True)
    o_ref[...] = (acc_sc[...] / l_sc[...]).astype(BF16)


def _attention(q2d, k3d, v3d, c, *, tq):
    n_seq, seq, att_w = k3d.shape
    n_heads = att_w // LANES
    nq = seq // tq
    kernel = functools.partial(_attn_kernel, tq=tq)
    return pl.pallas_call(
        kernel,
        grid=(n_seq, n_heads, nq),
        in_specs=[
            pl.BlockSpec((tq, LANES), lambda b, h, i: (b * nq + i, h)),
            pl.BlockSpec((1, seq, LANES), lambda b, h, i: (b, 0, h)),
            pl.BlockSpec((1, seq, LANES), lambda b, h, i: (b, 0, h)),
            pl.BlockSpec((1, n_heads, seq), lambda b, h, i: (b, 0, 0)),
        ],
        out_specs=pl.BlockSpec((tq, LANES), lambda b, h, i: (b * nq + i, h)),
        out_shape=jax.ShapeDtypeStruct((n_seq * seq, att_w), BF16),
        scratch_shapes=[pltpu.VMEM((tq, 1), F32), pltpu.VMEM((tq, 1), F32),
                        pltpu.VMEM((tq, LANES), F32)],
        compiler_params=_params(3),
        name="fox_attention_prompt",
    )(q2d, k3d, v3d, c)


def _attn_sample_kernel(q_ref, kp_ref, vp_ref, kn_ref, vn_ref, c_ref, o_ref, *, past):
    h = pl.program_id(1)
    q = q_ref[...]
    t = q.shape[0]
    c_row = c_ref[0, pl.ds(h, 1), :]
    nt = (((1,), (1,)), ((), ()))
    s_p = lax.dot_general(q, kp_ref[0].astype(BF16), nt, preferred_element_type=F32) - c_row[:, :past]
    s_n = lax.dot_general(q, kn_ref[...], nt, preferred_element_type=F32) - c_row[:, past:past + t]
    row = lax.broadcasted_iota(jnp.int32, (t, t), 0)
    colk = lax.broadcasted_iota(jnp.int32, (t, t), 1)
    s_n = jnp.where(colk <= row, s_n, NEG_INF)
    m = jnp.maximum(jnp.max(s_p, axis=1, keepdims=True), jnp.max(s_n, axis=1, keepdims=True))
    p_p = jnp.exp(s_p - m)
    p_n = jnp.exp(s_n - m)
    l = jnp.sum(p_p, axis=1, keepdims=True) + jnp.sum(p_n, axis=1, keepdims=True)
    acc = jnp.dot(p_p.astype(BF16), vp_ref[0].astype(BF16), preferred_element_type=F32)
    acc = acc + jnp.dot(p_n.astype(BF16), vn_ref[...], preferred_element_type=F32)
    o_ref[...] = (acc / l).astype(BF16)


def _attention_sample(q2d, k_new, v_new, k_past, v_past, c, *, t):
    n_seq, past, att_w = k_past.shape
    n_heads = att_w // LANES
    kernel = functools.partial(_attn_sample_kernel, past=past)
    new_spec = pl.BlockSpec((t, LANES), lambda b, h: (b, h))
    past_spec = pl.BlockSpec((1, past, LANES), lambda b, h: (b, 0, h))
    return pl.pallas_call(
        kernel,
        grid=(n_seq, n_heads),
        in_specs=[new_spec, past_spec, past_spec, new_spec, new_spec,
                  pl.BlockSpec((1, n_heads, c.shape[2]), lambda b, h: (b, 0, 0))],
        out_specs=new_spec,
        out_shape=jax.ShapeDtypeStruct((n_seq * t, att_w), BF16),
        compiler_params=_params(2),
        name="fox_attention_sample",
    )(q2d, k_past, v_past, k_new, v_new, c)


def _mix_kernel(z_ref, o_ref, ga_ref, gb_ref, x_ref, pw_ref, wa_ref, wo_ref, y_ref):
    a_out = jnp.dot(z_ref[...], pw_ref[...], preferred_element_type=F32)
    b_out = jnp.dot(o_ref[...], wa_ref[...], preferred_element_type=F32)
    mix = ga_ref[...].astype(F32) * a_out + gb_ref[...].astype(F32) * b_out
    y_ref[...] = x_ref[...] + jnp.dot(mix.astype(BF16), wo_ref[...], preferred_element_type=F32)


def _mix(z, o, ga, gb, x2d, pw, wa, wo, *, tm):
    m, d_model = x2d.shape

    def row(width):
        return pl.BlockSpec((tm, width), lambda i: (i, 0))

    return pl.pallas_call(
        _mix_kernel,
        grid=(m // tm,),
        in_specs=[row(z.shape[1]), row(o.shape[1]), row(d_model), row(d_model), row(d_model),
                  _const_spec(pw.shape), _const_spec(wa.shape), _const_spec(wo.shape)],
        out_specs=row(d_model),
        out_shape=jax.ShapeDtypeStruct((m, d_model), F32),
        compiler_params=_params(1),
        name="mix_out",
    )(z, o, ga, gb, x2d, pw, wa, wo)


def _ffn_kernel(x_ref, g_ref, wg_ref, wu_ref, wd_ref, cw_ref, cb_ref, st_ref,
                y_ref, nf_ref, h_ref, gbuf, carry, *, tiles_per_seq, row_step, halo, rc):
    i = pl.program_id(0)
    j = pl.program_id(1)
    tm = x_ref.shape[0]
    tf = wg_ref.shape[1]
    n_chunks = tm // rc
    cols = pl.ds(pl.multiple_of(j * tf, tf), tf)

    @pl.when(j == 0)
    def _():
        for c in range(n_chunks):
            rows = pl.ds(c * rc, rc)
            x = x_ref[rows, :]
            ms = jnp.mean(x * x, axis=-1, keepdims=True)
            h_ref[rows, :] = ((x * lax.rsqrt(ms + EPS)) * g_ref[...]).astype(BF16)
            y_ref[rows, :] = x

    first_in_seq = (i % tiles_per_seq) == 0

    @pl.when(first_in_seq)
    def _():
        gbuf[0:halo, :] = st_ref[0]

    @pl.when(jnp.logical_not(first_in_seq))
    def _():
        gbuf[0:halo, :] = carry[:, cols]

    w0 = cw_ref[0:1, :]
    w1 = cw_ref[1:2, :]
    w2 = cw_ref[2:3, :]
    for c in range(n_chunks):
        rows = pl.ds(c * rc, rc)
        hb = h_ref[rows, :]
        g = jnp.dot(hb, wg_ref[...], preferred_element_type=F32)
        up = jnp.dot(hb, wu_ref[...], preferred_element_type=F32)
        gbuf[pl.ds(halo + c * rc, rc), :] = g
        g2 = gbuf[pl.ds(halo + c * rc - 2 * row_step, rc), :]
        g1 = gbuf[pl.ds(halo + c * rc - row_step, rc), :]
        gc = w0 * g2 + w1 * g1 + w2 * g + cb_ref[...]
        act = ((gc * _sigmoid(gc)) * up).astype(BF16)
        y_ref[rows, :] += jnp.dot(act, wd_ref[...], preferred_element_type=F32)

    tail = gbuf[tm:tm + halo, :]
    carry[:, cols] = tail
    nf_ref[0] = tail


def _ffn(x2d, norm_g, wg, wu, wd, conv_w, conv_b, state, *, n_seq, row_step, tm, tf, rc):
    m, d_model = x2d.shape
    d_ff = wg.shape[1]
    halo = state.shape[1]
    tiles_per_seq = (m // n_seq) // tm
    kernel = functools.partial(_ffn_kernel, tiles_per_seq=tiles_per_seq, row_step=row_step,
                               halo=halo, rc=rc)
    y, new_state = pl.pallas_call(
        kernel,
        grid=(m // tm, d_ff // tf),
        in_specs=[
            pl.BlockSpec((tm, d_model), lambda i, j: (i, 0)),
            _const_spec((1, d_model)),
            pl.BlockSpec((d_model, tf), lambda i, j: (0, j)),
            pl.BlockSpec((d_model, tf), lambda i, j: (0, j)),
            pl.BlockSpec((tf, d_model), lambda i, j: (j, 0)),
            pl.BlockSpec((SUBLANES, tf), lambda i, j: (0, j)),
            pl.BlockSpec((1, tf), lambda i, j: (0, j)),
            pl.BlockSpec((1, halo, tf), lambda i, j: (i // tiles_per_seq, 0, j)),
        ],
        out_specs=(
            pl.BlockSpec((tm, d_model), lambda i, j: (i, 0)),
            pl.BlockSpec((1, halo, tf), lambda i, j: (i, 0, j)),
        ),
        out_shape=(jax.ShapeDtypeStruct((m, d_model), F32),
                   jax.ShapeDtypeStruct((m // tm, halo, d_ff), F32)),
        scratch_shapes=[pltpu.VMEM((tm, d_model), BF16),
                        pltpu.VMEM((halo + tm, tf), F32),
                        pltpu.VMEM((halo, d_ff), F32)],
        compiler_params=_params(2),
        name="conv_ffn",
    )(x2d, norm_g, wg, wu, wd, conv_w, conv_b, state)
    return y, new_state[tiles_per_seq - 1::tiles_per_seq]


def _prep_weights(norm_mix_g, w_in, b_forget, conv_dw_w, conv_dw_b, conv_ln_g, conv_ln_b,
                  conv_pw_out, q_norm_g, k_norm_g, w_att_out, w_out, norm_ffn_g, w_up,
                  ffn_dw_w, ffn_dw_b, w_down, *, n_heads):
    d_model = w_in.shape[0]
    d_conv = conv_dw_w.shape[1]
    att_w = w_att_out.shape[0]
    d_ff = w_down.shape[0]
    o0 = 2 * d_conv
    o3 = o0 + 3 * att_w
    o4 = o3 + n_heads
    n_glu = d_conv // GLU_HALF
    wa = w_in[:, :d_conv].reshape(d_model, n_glu, 1, GLU_HALF)
    wb = w_in[:, d_conv:o0].reshape(d_model, n_glu, 1, GLU_HALF)
    w_glu = jnp.concatenate([wa, wb], axis=2).reshape(d_model, o0)
    w_tiles = jnp.concatenate([w_glu, w_in[:, o0:o3], w_in[:, o4:]], axis=1).astype(BF16)
    wf_pad = jnp.pad(w_in[:, o3:o4], ((0, 0), (0, LANES - n_heads))).astype(BF16)
    bf_pad = jnp.pad(b_forget.reshape(1, n_heads), ((0, 0), (0, LANES - n_heads)))
    n_taps = conv_dw_w.shape[0]
    return dict(
        norm_mix_g=norm_mix_g.reshape(1, d_model), w_tiles=w_tiles, wf_pad=wf_pad, bf_pad=bf_pad,
        q_gain=q_norm_g.reshape(1, LANES), k_gain=k_norm_g.reshape(1, LANES),
        conv_w=jnp.pad(conv_dw_w, ((0, -n_taps % SUBLANES), (0, 0))),
        conv_b=conv_dw_b.reshape(1, d_conv), ln_g=conv_ln_g.reshape(1, d_conv),
        ln_b=conv_ln_b.reshape(1, d_conv),
        pw=conv_pw_out.astype(BF16), wa=w_att_out.astype(BF16), wo=w_out.astype(BF16),
        norm_ffn_g=norm_ffn_g.reshape(1, d_model),
        wg=w_up[:, :d_ff].astype(BF16), wu=w_up[:, d_ff:].astype(BF16), wd=w_down.astype(BF16),
        ffn_w=jnp.pad(ffn_dw_w, ((0, SUBLANES - ffn_dw_w.shape[0]), (0, 0))),
        ffn_b=ffn_dw_b.reshape(1, d_ff), n_taps=n_taps, n_ffn_taps=ffn_dw_w.shape[0],
        d_conv=d_conv, att_w=att_w, d_ff=d_ff,
    )


def _time_major(a, n_seq, t):
    return a.reshape(n_seq, t, -1).transpose(1, 0, 2).reshape(n_seq * t, -1)


def _seq_major(a, n_seq, t):
    return a.reshape(t, n_seq, -1).transpose(1, 0, 2).reshape(n_seq * t, -1)


def _layer_prompt(x, w, *, n_heads):
    n_seq, seq, d_model = x.shape
    m = n_seq * seq
    d_conv, att_w, d_ff = w["d_conv"], w["att_w"], w["d_ff"]
    x2d = x.reshape(m, d_model)
    tm = min(1024, seq)
    u, q, kf, kb, vf, vb, ga, gb, lf, lft = _in_proj(
        x2d, w["norm_mix_g"], w["w_tiles"], w["wf_pad"], w["bf_pad"], w["q_gain"], w["k_gain"],
        n_seq=n_seq, d_conv=d_conv, att_w=att_w, n_heads=n_heads, tm=tm, rc=min(256, tm))
    c = _cumsum(lft)
    conv_halo = 32
    ctx = jnp.zeros((n_seq, conv_halo, d_conv), F32)
    z = _conv(u, ctx, w["conv_w"], w["conv_b"], w["ln_g"], w["ln_b"], n_seq=n_seq,
              n_taps=w["n_taps"], row_step=1, tc=min(512, seq), rb_rows=64)
    o = _attention(q, kb.reshape(n_seq, seq, att_w), vb.reshape(n_seq, seq, att_w), c,
                   tq=min(512, seq))
    x1 = _mix(z, o, ga, gb, x2d, w["pw"], w["wa"], w["wo"], tm=min(512, seq))
    state = jnp.zeros((n_seq, SUBLANES, d_ff), F32)
    y, new_state = _ffn(x1, w["norm_ffn_g"], w["wg"], w["wu"], w["wd"], w["ffn_w"], w["ffn_b"],
                        state, n_seq=n_seq, row_step=1, tm=tm, tf=512, rc=min(256, tm))
    n_keep = w["n_taps"] - 1
    new_conv = u.reshape(n_seq, seq, d_conv)[:, seq - n_keep:]
    new_ffn = new_state[:, SUBLANES - (w["n_ffn_taps"] - 1):]
    return (y.reshape(n_seq, seq, d_model), new_conv, new_ffn,
            kf.reshape(n_seq, seq, n_heads, LANES), vf.reshape(n_seq, seq, n_heads, LANES),
            lf.reshape(n_seq, seq, n_heads))


def _layer_sample(x, conv_buf, ffn_buf, past_k, past_v, past_logf, w, *, n_heads):
    n_seq, t, d_model = x.shape
    m = n_seq * t
    past = past_k.shape[1]
    d_conv, att_w, d_ff = w["d_conv"], w["att_w"], w["d_ff"]
    x2d = x.reshape(m, d_model)
    u, q, kf, kb, vf, vb, ga, gb, lf, lft = _in_proj(
        x2d, w["norm_mix_g"], w["w_tiles"], w["wf_pad"], w["bf_pad"], w["q_gain"], w["k_gain"],
        n_seq=1, d_conv=d_conv, att_w=att_w, n_heads=n_heads, tm=m, rc=m)
    lft_new = lft.reshape(n_heads, n_seq, t).transpose(1, 0, 2)
    pad = -(past + t) % LANES
    lf_all = jnp.concatenate([past_logf.astype(F32).transpose(0, 2, 1), lft_new,
                              jnp.zeros((n_seq, n_heads, pad), F32)], axis=2)
    c = _cumsum(lf_all)
    n_keep = w["n_taps"] - 1
    ctx = conv_buf.astype(F32).transpose(1, 0, 2).reshape(1, n_keep * n_seq, d_conv)
    z_tm = _conv(_time_major(u, n_seq, t), ctx, w["conv_w"], w["conv_b"], w["ln_g"], w["ln_b"],
                 n_seq=1, n_taps=w["n_taps"], row_step=n_seq, tc=m, rb_rows=64)
    z = _seq_major(z_tm, n_seq, t)
    o = _attention_sample(q, kb, vb, past_k.reshape(n_seq, past, att_w),
                          past_v.reshape(n_seq, past, att_w), c, t=t)
    x1 = _mix(z, o, ga, gb, x2d, w["pw"], w["wa"], w["wo"], tm=m)
    n_fk = w["n_ffn_taps"] - 1
    state = ffn_buf.astype(F32).transpose(1, 0, 2).reshape(1, n_fk * n_seq, d_ff)
    y_tm, new_state = _ffn(_time_major(x1, n_seq, t), w["norm_ffn_g"], w["wg"], w["wu"], w["wd"],
                           w["ffn_w"], w["ffn_b"], state, n_seq=1, row_step=n_seq, tm=m, tf=512,
                           rc=m)
    y = _seq_major(y_tm, n_seq, t)
    new_conv = jnp.concatenate([conv_buf.astype(F32), u.reshape(n_seq, t, d_conv)],
                               axis=1)[:, -n_keep:]
    new_ffn = new_state.reshape(n_fk, n_seq, d_ff).transpose(1, 0, 2)
    return (y.reshape(n_seq, t, d_model), new_conv, new_ffn,
            kf.reshape(n_seq, t, n_heads, LANES), vf.reshape(n_seq, t, n_heads, LANES),
            lf.reshape(n_seq, t, n_heads))


def kernel(x_prompt, x_sample, cache_k, cache_v, cache_logf, state_conv, state_ffn, norm_mix_g, w_in, b_forget, conv_dw_w, conv_dw_b, conv_ln_g, conv_ln_b, conv_pw_out, q_norm_g, k_norm_g, w_att_out, w_out, norm_ffn_g, w_up, ffn_dw_w, ffn_dw_b, w_down):
    depth = w_in.shape[0]
    n_heads = cache_k.shape[3]
    assert cache_k.shape[4] == LANES
    xp, xs = x_prompt, x_sample
    outs_p, outs_s = [], []
    for l in range(depth):
        w = _prep_weights(norm_mix_g[l], w_in[l], b_forget[l], conv_dw_w[l], conv_dw_b[l],
                          conv_ln_g[l], conv_ln_b[l], conv_pw_out[l], q_norm_g[l], k_norm_g[l],
                          w_att_out[l], w_out[l], norm_ffn_g[l], w_up[l], ffn_dw_w[l],
                          ffn_dw_b[l], w_down[l], n_heads=n_heads)
        xp, *rest_p = _layer_prompt(xp, w, n_heads=n_heads)
        xs, *rest_s = _layer_sample(xs, state_conv[l], state_ffn[l], cache_k[l], cache_v[l],
                                    cache_logf[l], w, n_heads=n_heads)
        outs_p.append(rest_p)
        outs_s.append(rest_s)

    def stack(outs, idx):
        return jnp.stack([o[idx] for o in outs])

    return (xp, xs,
            stack(outs_p, 2), stack(outs_p, 3), stack(outs_p, 4), stack(outs_p, 0), stack(outs_p, 1),
            stack(outs_s, 2), stack(outs_s, 3), stack(outs_s, 4), stack(outs_s, 0), stack(outs_s, 1))
```

```python
import functools

import jax
import jax.numpy as jnp
from jax import lax
from jax.experimental import pallas as pl
from jax.experimental.pallas import tpu as pltpu

F32 = jnp.float32
BF16 = jnp.bfloat16
EPS = 1e-6
NEG_INF = -1e30
LOG2E = 1.4426950408889634
LANES = 128
SUBLANES = 8
VMEM_LIMIT_BYTES = 60 * 1024 * 1024
N_TILE = 512
GLU_HALF = N_TILE // 2


def _params(n_axes, vmem=VMEM_LIMIT_BYTES):
    return pltpu.CompilerParams(dimension_semantics=("arbitrary",) * n_axes,
                                vmem_limit_bytes=vmem)


def _const_spec(shape):
    nd = len(shape)
    return pl.BlockSpec(shape, lambda *_: (0,) * nd, pipeline_mode=pl.Buffered(1))


def _sigmoid(x):
    return jax.nn.sigmoid(x)


def _in_proj_kernel(x_ref, g_ref, w_ref, wf_ref, bf_ref, qg_ref, kg_ref,
                    u_ref, q_ref, kf_ref, vf_ref, ga_ref, gb_ref,
                    lf_ref, lft_ref, h_ref, *, rc, n_glu, n_att, n_gate, n_heads, scale):
    j = pl.program_id(1)
    tm = x_ref.shape[0]
    n_chunks = tm // rc
    j_q = n_glu
    j_k = j_q + n_att
    j_v = j_k + n_att
    j_ga = j_v + n_att
    j_gb = j_ga + n_gate
    heads_per_tile = N_TILE // LANES

    @pl.when(j == 0)
    def _():
        for c in range(n_chunks):
            rows = pl.ds(c * rc, rc)
            x = x_ref[rows, :]
            ms = jnp.mean(x * x, axis=-1, keepdims=True)
            hb = ((x * lax.rsqrt(ms + EPS)) * g_ref[...]).astype(BF16)
            h_ref[rows, :] = hb
            f = jnp.dot(hb, wf_ref[...], preferred_element_type=F32) + bf_ref[...]
            lf = jnp.minimum(f, 0.0) - jnp.log1p(jnp.exp(-jnp.abs(f)))
            lf_ref[rows, :] = lf[:, :n_heads]
            lft_ref[0, :, rows] = lf.T[:n_heads, :]

    def chunks(epilogue):
        for c in range(n_chunks):
            rows = pl.ds(c * rc, rc)
            acc = jnp.dot(h_ref[rows, :], w_ref[...], preferred_element_type=F32)
            epilogue(rows, acc)

    def head_norm(acc, gain_ref, hh):
        xh = acc[:, hh * LANES:(hh + 1) * LANES]
        ms = jnp.mean(xh * xh, axis=-1, keepdims=True)
        return (xh * lax.rsqrt(ms + EPS)) * gain_ref[...]

    @pl.when(j < j_q)
    def _():
        def ep(rows, acc):
            u_ref[rows, :] = acc[:, :GLU_HALF] * _sigmoid(acc[:, GLU_HALF:])
        chunks(ep)

    @pl.when((j >= j_q) & (j < j_k))
    def _():
        def ep(rows, acc):
            for hh in range(heads_per_tile):
                qn = head_norm(acc, qg_ref, hh) * scale
                q_ref[rows, hh * LANES:(hh + 1) * LANES] = qn.astype(BF16)
        chunks(ep)

    @pl.when((j >= j_k) & (j < j_v))
    def _():
        def ep(rows, acc):
            for hh in range(heads_per_tile):
                kn = head_norm(acc, kg_ref, hh)
                kf_ref[rows, hh * LANES:(hh + 1) * LANES] = kn
        chunks(ep)

    @pl.when((j >= j_v) & (j < j_ga))
    def _():
        def ep(rows, acc):
            vf_ref[rows, :] = acc
        chunks(ep)

    @pl.when((j >= j_ga) & (j < j_gb))
    def _():
        def ep(rows, acc):
            ga_ref[rows, :] = _sigmoid(acc).astype(BF16)
        chunks(ep)

    @pl.when(j >= j_gb)
    def _():
        def ep(rows, acc):
            gb_ref[rows, :] = _sigmoid(acc).astype(BF16)
        chunks(ep)


def _in_proj(x2d, norm_g, w_tiles, wf_pad, bf_pad, q_gain, k_gain, *, n_seq, d_conv, att_w,
             n_heads, tm, rc):
    m, d_model = x2d.shape
    seq = m // n_seq
    tiles_per_seq = seq // tm
    n_glu = d_conv // GLU_HALF
    n_att = att_w // N_TILE
    n_gate = d_model // N_TILE
    n_tiles = n_glu + 3 * n_att + 2 * n_gate
    scale = float(LANES) ** -0.5 * LOG2E
    j_q, j_k, j_v = n_glu, n_glu + n_att, n_glu + 2 * n_att
    j_ga = n_glu + 3 * n_att
    j_gb = j_ga + n_gate

    def col(start, count):
        return lambda i, j: (i, jnp.clip(j - start, 0, count - 1))

    kernel = functools.partial(_in_proj_kernel, rc=rc, n_glu=n_glu, n_att=n_att,
                               n_gate=n_gate, n_heads=n_heads, scale=scale)
    out_shape = (
        jax.ShapeDtypeStruct((m, d_conv), F32),
        jax.ShapeDtypeStruct((m, att_w), BF16),
        jax.ShapeDtypeStruct((m, att_w), F32),
        jax.ShapeDtypeStruct((m, att_w), F32),
        jax.ShapeDtypeStruct((m, d_model), BF16),
        jax.ShapeDtypeStruct((m, d_model), BF16),
        jax.ShapeDtypeStruct((m, n_heads), F32),
        jax.ShapeDtypeStruct((n_seq, n_heads, seq), F32),
    )
    out_specs = (
        pl.BlockSpec((tm, GLU_HALF), col(0, n_glu)),
        pl.BlockSpec((tm, N_TILE), col(j_q, n_att)),
        pl.BlockSpec((tm, N_TILE), col(j_k, n_att)),
        pl.BlockSpec((tm, N_TILE), col(j_v, n_att)),
        pl.BlockSpec((tm, N_TILE), col(j_ga, n_gate)),
        pl.BlockSpec((tm, N_TILE), col(j_gb, n_gate)),
        pl.BlockSpec((tm, n_heads), lambda i, j: (i, 0)),
        pl.BlockSpec((1, n_heads, tm), lambda i, j: (i // tiles_per_seq, 0, i % tiles_per_seq)),
    )
    in_specs = [
        pl.BlockSpec((tm, d_model), lambda i, j: (i, 0)),
        _const_spec((1, d_model)),
        pl.BlockSpec((d_model, N_TILE), lambda i, j: (0, j)),
        _const_spec((d_model, LANES)),
        _const_spec((1, LANES)),
        _const_spec((1, LANES)),
        _const_spec((1, LANES)),
    ]
    return pl.pallas_call(
        kernel,
        grid=(m // tm, n_tiles),
        in_specs=in_specs,
        out_specs=out_specs,
        out_shape=out_shape,
        scratch_shapes=[pltpu.VMEM((tm, d_model), BF16)],
        compiler_params=_params(2),
        name="in_proj",
    )(x2d, norm_g, w_tiles, wf_pad, bf_pad, q_gain, k_gain)


def _cumsum_kernel(lf_ref, tri_ref, c_ref):
    n_blocks = lf_ref.shape[2] // LANES
    run = jnp.zeros((lf_ref.shape[1], 1), F32)
    for blk in range(n_blocks):
        cols = slice(blk * LANES, (blk + 1) * LANES)
        cb = jnp.dot(lf_ref[0, :, cols], tri_ref[...], precision=lax.Precision.HIGHEST,
                     preferred_element_type=F32) + run
        c_ref[0, :, cols] = cb
        run = cb[:, LANES - 1:LANES]


def _cumsum(lft):
    n_seq, n_heads, length = lft.shape
    idx = jnp.arange(LANES)
    tri = (idx[:, None] <= idx[None, :]).astype(F32)
    return pl.pallas_call(
        _cumsum_kernel,
        grid=(n_seq,),
        in_specs=[pl.BlockSpec((1, n_heads, length), lambda b: (b, 0, 0)),
                  _const_spec((LANES, LANES))],
        out_specs=pl.BlockSpec((1, n_heads, length), lambda b: (b, 0, 0)),
        out_shape=jax.ShapeDtypeStruct(lft.shape, F32),
        compiler_params=_params(1),
        name="logf_cumsum",
    )(lft, tri)


def _conv_kernel(u_ref, ctx_ref, w_ref, b_ref, lg_ref, lb_ref, z_ref, xbuf, ybuf,
                 *, n_taps, row_step, halo, rb_rows):
    i = pl.program_id(1)
    tc, d_conv = u_ref.shape
    first_tap = halo - (n_taps - 1) * row_step

    @pl.when(i == 0)
    def _():
        xbuf[0:halo, :] = ctx_ref[0]

    @pl.when(i > 0)
    def _():
        xbuf[0:halo, :] = xbuf[tc:tc + halo, :]

    xbuf[halo:halo + tc, :] = u_ref[...]

    def lane_chunk(cc, carry):
        cols = pl.ds(pl.multiple_of(cc * LANES, LANES), LANES)
        for rb in range(tc // rb_rows):
            r0 = rb * rb_rows
            acc = jnp.broadcast_to(b_ref[:, cols], (rb_rows, LANES))
            for k in range(n_taps):
                xk = xbuf[pl.ds(r0 + first_tap + k * row_step, rb_rows), cols]
                acc = acc + w_ref[k:k + 1, cols] * xk
            ybuf[pl.ds(r0, rb_rows), cols] = acc
        return carry

    lax.fori_loop(0, d_conv // LANES, lane_chunk, 0)

    for rb in range(tc // rb_rows):
        rows = pl.ds(rb * rb_rows, rb_rows)
        y = ybuf[rows, :]
        mu = jnp.mean(y, axis=-1, keepdims=True)
        yc = y - mu
        var = jnp.mean(yc * yc, axis=-1, keepdims=True)
        zn = (yc * lax.rsqrt(var + EPS)) * lg_ref[...] + lb_ref[...]
        z_ref[rows, :] = (zn * _sigmoid(zn)).astype(BF16)


def _conv(u2d, ctx, w_pad, bias, ln_g, ln_b, *, n_seq, n_taps, row_step, tc, rb_rows):
    m, d_conv = u2d.shape
    halo = ctx.shape[1]
    tiles_per_seq = (m // n_seq) // tc
    kernel = functools.partial(_conv_kernel, n_taps=n_taps, row_step=row_step, halo=halo,
                               rb_rows=rb_rows)
    return pl.pallas_call(
        kernel,
        grid=(n_seq, tiles_per_seq),
        in_specs=[
            pl.BlockSpec((tc, d_conv), lambda b, i: (b * tiles_per_seq + i, 0)),
            pl.BlockSpec((1, halo, d_conv), lambda b, i: (b, 0, 0)),
            _const_spec(w_pad.shape),
            _const_spec((1, d_conv)),
            _const_spec((1, d_conv)),
            _const_spec((1, d_conv)),
        ],
        out_specs=pl.BlockSpec((tc, d_conv), lambda b, i: (b * tiles_per_seq + i, 0)),
        out_shape=jax.ShapeDtypeStruct((m, d_conv), BF16),
        scratch_shapes=[pltpu.VMEM((halo + tc, d_conv), F32), pltpu.VMEM((tc, d_conv), F32)],
        compiler_params=_params(2),
        name="conv_branch",
    )(u2d, ctx, w_pad, bias, ln_g, ln_b)


CONV_BLOCK = 64
X_PITCH = CONV_BLOCK + 4
Y_PITCH = CONV_BLOCK + SUBLANES


def _conv_prompt_kernel(u_ref, ctx_ref, w_ref, b_ref, lg_ref, lb_ref, z_ref, xbuf, ybuf,
                        *, n_taps, r_group):
    i = pl.program_id(1)
    tc, d_conv = u_ref.shape
    n_slabs = d_conv // LANES
    n_blocks = tc // CONV_BLOCK

    @pl.when(i == 0)
    def _():
        for cc in range(n_slabs):
            xbuf[cc, 0:CONV_BLOCK, :] = ctx_ref[0, :, cc * LANES:(cc + 1) * LANES]

    @pl.when(i > 0)
    def _():
        for cc in range(n_slabs):
            xbuf[cc, 0:CONV_BLOCK, :] = xbuf[cc, pl.ds(n_blocks * X_PITCH, CONV_BLOCK), :]

    for cc in range(n_slabs):
        for blk in range(n_blocks):
            xbuf[cc, pl.ds((blk + 1) * X_PITCH, CONV_BLOCK), :] = (
                u_ref[blk * CONV_BLOCK:(blk + 1) * CONV_BLOCK, cc * LANES:(cc + 1) * LANES])

    def slab(cc, carry):
        bias = jnp.broadcast_to(b_ref[cc], (SUBLANES, LANES))
        for r0 in range(0, CONV_BLOCK, r_group):
            accs = [bias] * r_group
            for k in range(n_taps):
                wk = jnp.broadcast_to(w_ref[cc, k:k + 1, :], (SUBLANES, LANES))
                for rr in range(r_group):
                    d = r0 + rr - (n_taps - 1) + k
                    start = X_PITCH + d if d >= 0 else CONV_BLOCK + d
                    xk = xbuf[cc, pl.ds(start, SUBLANES, stride=X_PITCH), :]
                    accs[rr] = accs[rr] + wk * xk
            for rr in range(r_group):
                ybuf[cc, pl.ds(r0 + rr, SUBLANES, stride=Y_PITCH), :] = accs[rr]
        return carry

    lax.fori_loop(0, n_slabs, slab, 0)

    for blk in range(n_blocks):
        y = jnp.concatenate([ybuf[cc, pl.ds(blk * Y_PITCH, CONV_BLOCK), :]
                             for cc in range(n_slabs)], axis=1)
        mu = jnp.mean(y, axis=-1, keepdims=True)
        yc = y - mu
        var = jnp.mean(yc * yc, axis=-1, keepdims=True)
        zn = (yc * lax.rsqrt(var + EPS)) * lg_ref[...] + lb_ref[...]
        z_ref[blk * CONV_BLOCK:(blk + 1) * CONV_BLOCK, :] = (zn * _sigmoid(zn)).astype(BF16)


def _conv_prompt(u2d, ctx, w_slabs, b_slabs, ln_g, ln_b, *, n_seq, n_taps):
    m, d_conv = u2d.shape
    n_slabs = d_conv // LANES
    tc = CONV_BLOCK * SUBLANES
    tiles_per_seq = (m // n_seq) // tc
    kernel = functools.partial(_conv_prompt_kernel, n_taps=n_taps, r_group=SUBLANES)
    return pl.pallas_call(
        kernel,
        grid=(n_seq, tiles_per_seq),
        in_specs=[
            pl.BlockSpec((tc, d_conv), lambda b, i: (b * tiles_per_seq + i, 0)),
            pl.BlockSpec((1, CONV_BLOCK, d_conv), lambda b, i: (b, 0, 0)),
            _const_spec(w_slabs.shape),
            _const_spec(b_slabs.shape),
            _const_spec((1, d_conv)),
            _const_spec((1, d_conv)),
        ],
        out_specs=pl.BlockSpec((tc, d_conv), lambda b, i: (b * tiles_per_seq + i, 0)),
        out_shape=jax.ShapeDtypeStruct((m, d_conv), BF16),
        scratch_shapes=[
            pltpu.VMEM((n_slabs, (SUBLANES + 1) * X_PITCH + 4, LANES), F32),
            pltpu.VMEM((n_slabs, SUBLANES * Y_PITCH, LANES), F32)],
        compiler_params=_params(2),
        name="conv_branch_prompt",
    )(u2d, ctx, w_slabs, b_slabs, ln_g, ln_b)


def _attn_kernel(q_ref, k_ref, v_ref, c_ref, o_ref, *, tq):
    h = pl.program_id(1)
    seq = q_ref.shape[1]
    nq = seq // tq
    c2 = c_ref[0, pl.ds(h, 1), :] * LOG2E
    row = lax.broadcasted_iota(jnp.int32, (tq, tq), 0)
    colk = lax.broadcasted_iota(jnp.int32, (tq, tq), 1)
    causal = colk <= row
    nt = (((1,), (1,)), ((), ()))
    qs = [q_ref[0, i * tq:(i + 1) * tq, :] for i in range(nq)]
    m = [None] * nq
    l = [None] * nq
    acc = [None] * nq
    for j in range(nq):
        keys = slice(j * tq, (j + 1) * tq)
        kj = k_ref[0, keys, :].astype(BF16)
        vj = v_ref[0, keys, :].astype(BF16)
        cj = c2[:, keys]
        for i in range(j, nq):
            s = lax.dot_general(qs[i], kj, nt, preferred_element_type=F32) - cj
            if i == j:
                s = jnp.where(causal, s, NEG_INF)
            s_max = jnp.max(s, axis=1, keepdims=True)
            if j == 0:
                m[i] = s_max
                p = jnp.exp2(s - s_max)
                l[i] = jnp.sum(p, axis=1, keepdims=True)
                acc[i] = jnp.dot(p.astype(BF16), vj, preferred_element_type=F32)
            else:
                m_new = jnp.maximum(m[i], s_max)
                alpha = jnp.exp2(m[i] - m_new)
                p = jnp.exp2(s - m_new)
                l[i] = alpha * l[i] + jnp.sum(p, axis=1, keepdims=True)
                acc[i] = alpha * acc[i] + jnp.dot(p.astype(BF16), vj,
                                                  preferred_element_type=F32)
                m[i] = m_new
        o_ref[0, keys, :] = (acc[j] / l[j]).astype(BF16)


def _attention(q3d, k3d, v3d, c, *, tq):
    n_seq, seq, att_w = k3d.shape
    n_heads = att_w // LANES
    kernel = functools.partial(_attn_kernel, tq=tq)
    head_spec = pl.BlockSpec((1, seq, LANES), lambda b, h: (b, 0, h))
    return pl.pallas_call(
        kernel,
        grid=(n_seq, n_heads),
        in_specs=[head_spec, head_spec, head_spec,
                  pl.BlockSpec((1, n_heads, seq), lambda b, h: (b, 0, 0))],
        out_specs=head_spec,
        out_shape=jax.ShapeDtypeStruct((n_seq, seq, att_w), BF16),
        compiler_params=_params(2),
        name="fox_attention_prompt",
    )(q3d, k3d, v3d, c)


def _attn_sample_kernel(q_ref, kp_ref, vp_ref, kn_ref, vn_ref, c_ref, o_ref, *, past, n_heads):
    t = q_ref.shape[0]
    nt = (((1,), (1,)), ((), ()))
    row = lax.broadcasted_iota(jnp.int32, (t, t), 0)
    colk = lax.broadcasted_iota(jnp.int32, (t, t), 1)
    causal = colk <= row
    for h in range(n_heads):
        lanes = slice(h * LANES, (h + 1) * LANES)
        q = q_ref[:, lanes]
        c2 = c_ref[0, h:h + 1, :] * LOG2E
        k_past = kp_ref[0, pl.ds(h, past, stride=n_heads), :].astype(BF16)
        v_past = vp_ref[0, pl.ds(h, past, stride=n_heads), :].astype(BF16)
        s_p = lax.dot_general(q, k_past, nt, preferred_element_type=F32) - c2[:, :past]
        s_n = lax.dot_general(q, kn_ref[:, lanes].astype(BF16), nt,
                              preferred_element_type=F32) - c2[:, past:past + t]
        s_n = jnp.where(causal, s_n, NEG_INF)
        m = jnp.maximum(jnp.max(s_p, axis=1, keepdims=True), jnp.max(s_n, axis=1, keepdims=True))
        p_p = jnp.exp2(s_p - m)
        p_n = jnp.exp2(s_n - m)
        l = jnp.sum(p_p, axis=1, keepdims=True) + jnp.sum(p_n, axis=1, keepdims=True)
        acc = jnp.dot(p_p.astype(BF16), v_past, preferred_element_type=F32)
        acc = acc + jnp.dot(p_n.astype(BF16), vn_ref[:, lanes].astype(BF16),
                            preferred_element_type=F32)
        o_ref[:, lanes] = (acc / l).astype(BF16)


def _attention_sample(q2d, k_new, v_new, k_past, v_past, c, *, t):
    n_seq, past_rows, _ = k_past.shape
    att_w = q2d.shape[1]
    n_heads = att_w // LANES
    kernel = functools.partial(_attn_sample_kernel, past=past_rows // n_heads, n_heads=n_heads)
    new_spec = pl.BlockSpec((t, att_w), lambda b: (b, 0))
    past_spec = pl.BlockSpec((1, past_rows, LANES), lambda b: (b, 0, 0))
    return pl.pallas_call(
        kernel,
        grid=(n_seq,),
        in_specs=[new_spec, past_spec, past_spec, new_spec, new_spec,
                  pl.BlockSpec((1, n_heads, c.shape[2]), lambda b: (b, 0, 0))],
        out_specs=new_spec,
        out_shape=jax.ShapeDtypeStruct((n_seq * t, att_w), BF16),
        compiler_params=_params(1),
        name="fox_attention_sample",
    )(q2d, k_past, v_past, k_new, v_new, c)


def _mix_kernel(z_ref, o_ref, ga_ref, gb_ref, x_ref, pw_ref, wa_ref, wo_ref, y_ref):
    a_out = jnp.dot(z_ref[...], pw_ref[...], preferred_element_type=F32)
    b_out = jnp.dot(o_ref[...], wa_ref[...], preferred_element_type=F32)
    mix = ga_ref[...].astype(F32) * a_out + gb_ref[...].astype(F32) * b_out
    y_ref[...] = x_ref[...] + jnp.dot(mix.astype(BF16), wo_ref[...], preferred_element_type=F32)


def _mix(z, o, ga, gb, x2d, pw, wa, wo, *, tm):
    m, d_model = x2d.shape

    def row(width):
        return pl.BlockSpec((tm, width), lambda i: (i, 0))

    return pl.pallas_call(
        _mix_kernel,
        grid=(m // tm,),
        in_specs=[row(z.shape[1]), row(o.shape[1]), row(d_model), row(d_model), row(d_model),
                  _const_spec(pw.shape), _const_spec(wa.shape), _const_spec(wo.shape)],
        out_specs=row(d_model),
        out_shape=jax.ShapeDtypeStruct((m, d_model), F32),
        compiler_params=_params(1),
        name="mix_out",
    )(z, o, ga, gb, x2d, pw, wa, wo)


def _ffn_kernel(x_ref, g_ref, wg_ref, wu_ref, wd_ref, cw_ref, cb_ref, st_ref,
                y_ref, nf_ref, h_ref, gbuf, carry, *, tiles_per_seq, row_step, halo, rc):
    i = pl.program_id(0)
    j = pl.program_id(1)
    tm = x_ref.shape[0]
    tf = wg_ref.shape[1]
    n_chunks = tm // rc
    cols = pl.ds(pl.multiple_of(j * tf, tf), tf)

    @pl.when(j == 0)
    def _():
        for c in range(n_chunks):
            rows = pl.ds(c * rc, rc)
            x = x_ref[rows, :]
            ms = jnp.mean(x * x, axis=-1, keepdims=True)
            h_ref[rows, :] = ((x * lax.rsqrt(ms + EPS)) * g_ref[...]).astype(BF16)
            y_ref[rows, :] = x

    first_in_seq = (i % tiles_per_seq) == 0

    @pl.when(first_in_seq)
    def _():
        gbuf[0:halo, :] = st_ref[0]

    @pl.when(jnp.logical_not(first_in_seq))
    def _():
        gbuf[0:halo, :] = carry[:, cols]

    w0 = cw_ref[0:1, :]
    w1 = cw_ref[1:2, :]
    w2 = cw_ref[2:3, :]
    for c in range(n_chunks):
        rows = pl.ds(c * rc, rc)
        hb = h_ref[rows, :]
        g = jnp.dot(hb, wg_ref[...], preferred_element_type=F32)
        up = jnp.dot(hb, wu_ref[...], preferred_element_type=F32)
        gbuf[pl.ds(halo + c * rc, rc), :] = g
        g2 = gbuf[pl.ds(halo + c * rc - 2 * row_step, rc), :]
        g1 = gbuf[pl.ds(halo + c * rc - row_step, rc), :]
        gc = w0 * g2 + w1 * g1 + w2 * g + cb_ref[...]
        act = ((gc * _sigmoid(gc)) * up).astype(BF16)
        y_ref[rows, :] += jnp.dot(act, wd_ref[...], preferred_element_type=F32)

    tail = gbuf[tm:tm + halo, :]
    carry[:, cols] = tail
    nf_ref[0] = tail


def _ffn(x2d, norm_g, w_up, wd, conv_w, conv_b, state, *, n_seq, row_step, tm, tf, rc):
    m, d_model = x2d.shape
    d_ff = wd.shape[0]
    n_ff_tiles = d_ff // tf
    halo = state.shape[1]
    tiles_per_seq = (m // n_seq) // tm
    kernel = functools.partial(_ffn_kernel, tiles_per_seq=tiles_per_seq, row_step=row_step,
                               halo=halo, rc=rc)
    y, new_state = pl.pallas_call(
        kernel,
        grid=(m // tm, n_ff_tiles),
        in_specs=[
            pl.BlockSpec((tm, d_model), lambda i, j: (i, 0)),
            _const_spec((1, d_model)),
            pl.BlockSpec((d_model, tf), lambda i, j: (0, j)),
            pl.BlockSpec((d_model, tf), lambda i, j: (0, n_ff_tiles + j)),
            pl.BlockSpec((tf, d_model), lambda i, j: (j, 0)),
            pl.BlockSpec((SUBLANES, tf), lambda i, j: (0, j)),
            pl.BlockSpec((1, tf), lambda i, j: (0, j)),
            pl.BlockSpec((1, halo, tf), lambda i, j: (i // tiles_per_seq, 0, j)),
        ],
        out_specs=(
            pl.BlockSpec((tm, d_model), lambda i, j: (i, 0)),
            pl.BlockSpec((1, halo, tf), lambda i, j: (i, 0, j)),
        ),
        out_shape=(jax.ShapeDtypeStruct((m, d_model), F32),
                   jax.ShapeDtypeStruct((m // tm, halo, d_ff), F32)),
        scratch_shapes=[pltpu.VMEM((tm, d_model), BF16),
                        pltpu.VMEM((halo + tm, tf), F32),
                        pltpu.VMEM((halo, d_ff), F32)],
        compiler_params=_params(2),
        name="conv_ffn",
    )(x2d, norm_g, w_up, w_up, wd, conv_w, conv_b, state)
    return y, new_state[tiles_per_seq - 1::tiles_per_seq]


def _prep_weights(norm_mix_g, w_in, b_forget, conv_dw_w, conv_dw_b, conv_ln_g, conv_ln_b,
                  conv_pw_out, q_norm_g, k_norm_g, w_att_out, w_out, norm_ffn_g, w_up,
                  ffn_dw_w, ffn_dw_b, w_down, *, n_heads):
    d_model = w_in.shape[0]
    d_conv = conv_dw_w.shape[1]
    att_w = w_att_out.shape[0]
    d_ff = w_down.shape[0]
    o0 = 2 * d_conv
    o3 = o0 + 3 * att_w
    o4 = o3 + n_heads
    n_glu = d_conv // GLU_HALF
    glu_cols = []
    for t in range(n_glu):
        glu_cols.append(w_in[:, t * GLU_HALF:(t + 1) * GLU_HALF])
        glu_cols.append(w_in[:, d_conv + t * GLU_HALF:d_conv + (t + 1) * GLU_HALF])
    w_tiles = jnp.concatenate(glu_cols + [w_in[:, o0:o3], w_in[:, o4:]], axis=1).astype(BF16)
    wf_pad = jnp.pad(w_in[:, o3:o4], ((0, 0), (0, LANES - n_heads))).astype(BF16)
    bf_pad = jnp.pad(b_forget.reshape(1, n_heads), ((0, 0), (0, LANES - n_heads)))
    n_taps = conv_dw_w.shape[0]
    conv_w = jnp.pad(conv_dw_w, ((0, -n_taps % SUBLANES), (0, 0)))
    n_slabs = d_conv // LANES
    return dict(
        norm_mix_g=norm_mix_g.reshape(1, d_model), w_tiles=w_tiles, wf_pad=wf_pad, bf_pad=bf_pad,
        q_gain=q_norm_g.reshape(1, LANES), k_gain=k_norm_g.reshape(1, LANES),
        conv_w=conv_w,
        conv_w_slabs=conv_w.reshape(-1, n_slabs, LANES).transpose(1, 0, 2),
        conv_b=conv_dw_b.reshape(1, d_conv), conv_b_slabs=conv_dw_b.reshape(n_slabs, 1, LANES),
        ln_g=conv_ln_g.reshape(1, d_conv), ln_b=conv_ln_b.reshape(1, d_conv),
        pw=conv_pw_out.astype(BF16), wa=w_att_out.astype(BF16), wo=w_out.astype(BF16),
        norm_ffn_g=norm_ffn_g.reshape(1, d_model),
        w_up=w_up.astype(BF16), wd=w_down.astype(BF16),
        ffn_w=jnp.pad(ffn_dw_w, ((0, SUBLANES - ffn_dw_w.shape[0]), (0, 0))),
        ffn_b=ffn_dw_b.reshape(1, d_ff), n_taps=n_taps, n_ffn_taps=ffn_dw_w.shape[0],
        d_conv=d_conv, att_w=att_w, d_ff=d_ff,
    )


def _time_major(a, n_seq, t):
    return a.reshape(n_seq, t, -1).transpose(1, 0, 2).reshape(n_seq * t, -1)


def _seq_major(a, n_seq, t):
    return a.reshape(t, n_seq, -1).transpose(1, 0, 2).reshape(n_seq * t, -1)


def _layer_prompt(x, w, *, n_heads):
    n_seq, seq, d_model = x.shape
    m = n_seq * seq
    d_conv, att_w, d_ff = w["d_conv"], w["att_w"], w["d_ff"]
    x2d = x.reshape(m, d_model)
    tm = min(1024, seq)
    u, q, kf, vf, ga, gb, lf, lft = _in_proj(
        x2d, w["norm_mix_g"], w["w_tiles"], w["wf_pad"], w["bf_pad"], w["q_gain"], w["k_gain"],
        n_seq=n_seq, d_conv=d_conv, att_w=att_w, n_heads=n_heads, tm=tm, rc=min(256, tm))
    c = _cumsum(lft)
    ctx = jnp.zeros((n_seq, CONV_BLOCK, d_conv), F32)
    z = _conv_prompt(u, ctx, w["conv_w_slabs"], w["conv_b_slabs"], w["ln_g"], w["ln_b"],
                     n_seq=n_seq, n_taps=w["n_taps"])
    o = _attention(q.reshape(n_seq, seq, att_w), kf.reshape(n_seq, seq, att_w),
                   vf.reshape(n_seq, seq, att_w), c, tq=min(512, seq)).reshape(m, att_w)
    x1 = _mix(z, o, ga, gb, x2d, w["pw"], w["wa"], w["wo"], tm=min(512, seq))
    state = jnp.zeros((n_seq, SUBLANES, d_ff), F32)
    y, new_state = _ffn(x1, w["norm_ffn_g"], w["w_up"], w["wd"], w["ffn_w"], w["ffn_b"],
                        state, n_seq=n_seq, row_step=1, tm=tm, tf=512, rc=min(256, tm))
    n_keep = w["n_taps"] - 1
    new_conv = u.reshape(n_seq, seq, d_conv)[:, seq - n_keep:]
    new_ffn = new_state[:, SUBLANES - (w["n_ffn_taps"] - 1):]
    return (y.reshape(n_seq, seq, d_model), new_conv, new_ffn,
            kf.reshape(n_seq, seq, n_heads, LANES), vf.reshape(n_seq, seq, n_heads, LANES),
            lf.reshape(n_seq, seq, n_heads))


def _layer_sample(x, conv_buf, ffn_buf, past_k, past_v, past_logf, w, *, n_heads):
    n_seq, t, d_model = x.shape
    m = n_seq * t
    past = past_k.shape[1]
    d_conv, att_w, d_ff = w["d_conv"], w["att_w"], w["d_ff"]
    x2d = x.reshape(m, d_model)
    u, q, kf, vf, ga, gb, lf, lft = _in_proj(
        x2d, w["norm_mix_g"], w["w_tiles"], w["wf_pad"], w["bf_pad"], w["q_gain"], w["k_gain"],
        n_seq=1, d_conv=d_conv, att_w=att_w, n_heads=n_heads, tm=m, rc=m)
    lft_new = lft.reshape(n_heads, n_seq, t).transpose(1, 0, 2)
    pad = -(past + t) % LANES
    lf_all = jnp.concatenate([past_logf.astype(F32).transpose(0, 2, 1), lft_new,
                              jnp.zeros((n_seq, n_heads, pad), F32)], axis=2)
    c = _cumsum(lf_all)
    n_keep = w["n_taps"] - 1
    ctx = conv_buf.astype(F32).transpose(1, 0, 2).reshape(1, n_keep * n_seq, d_conv)
    z_tm = _conv(_time_major(u, n_seq, t), ctx, w["conv_w"], w["conv_b"], w["ln_g"], w["ln_b"],
                 n_seq=1, n_taps=w["n_taps"], row_step=n_seq, tc=m, rb_rows=64)
    z = _seq_major(z_tm, n_seq, t)
    o = _attention_sample(q, kf, vf, past_k.reshape(n_seq, past * n_heads, LANES),
                          past_v.reshape(n_seq, past * n_heads, LANES), c, t=t)
    x1 = _mix(z, o, ga, gb, x2d, w["pw"], w["wa"], w["wo"], tm=m)
    n_fk = w["n_ffn_taps"] - 1
    state = ffn_buf.astype(F32).transpose(1, 0, 2).reshape(1, n_fk * n_seq, d_ff)
    y_tm, new_state = _ffn(_time_major(x1, n_seq, t), w["norm_ffn_g"], w["w_up"], w["wd"],
                           w["ffn_w"], w["ffn_b"], state, n_seq=1, row_step=n_seq, tm=m, tf=512,
                           rc=m)
    y = _seq_major(y_tm, n_seq, t)
    new_conv = jnp.concatenate([conv_buf.astype(F32), u.reshape(n_seq, t, d_conv)],
                               axis=1)[:, -n_keep:]
    new_ffn = new_state.reshape(n_fk, n_seq, d_ff).transpose(1, 0, 2)
    return (y.reshape(n_seq, t, d_model), new_conv, new_ffn,
            kf.reshape(n_seq, t, n_heads, LANES), vf.reshape(n_seq, t, n_heads, LANES),
            lf.reshape(n_seq, t, n_heads))


def kernel(x_prompt, x_sample, cache_k, cache_v, cache_logf, state_conv, state_ffn, norm_mix_g, w_in, b_forget, conv_dw_w, conv_dw_b, conv_ln_g, conv_ln_b, conv_pw_out, q_norm_g, k_norm_g, w_att_out, w_out, norm_ffn_g, w_up, ffn_dw_w, ffn_dw_b, w_down):
    depth = w_in.shape[0]
    n_heads = cache_k.shape[3]
    assert cache_k.shape[4] == LANES
    xp, xs = x_prompt, x_sample
    outs_p, outs_s = [], []
    for l in range(depth):
        w = _prep_weights(norm_mix_g[l], w_in[l], b_forget[l], conv_dw_w[l], conv_dw_b[l],
                          conv_ln_g[l], conv_ln_b[l], conv_pw_out[l], q_norm_g[l], k_norm_g[l],
                          w_att_out[l], w_out[l], norm_ffn_g[l], w_up[l], ffn_dw_w[l],
                          ffn_dw_b[l], w_down[l], n_heads=n_heads)
        xp, *rest_p = _layer_prompt(xp, w, n_heads=n_heads)
        xs, *rest_s = _layer_sample(xs, state_conv[l], state_ffn[l], cache_k[l], cache_v[l],
                                    cache_logf[l], w, n_heads=n_heads)
        outs_p.append(rest_p)
        outs_s.append(rest_s)

    def stack(outs, idx):
        return jnp.stack([o[idx] for o in outs])

    return (xp, xs,
            stack(outs_p, 2), stack(outs_p, 3), stack(outs_p, 4), stack(outs_p, 0), stack(outs_p, 1),
            stack(outs_s, 2), stack(outs_s, 3), stack(outs_s, 4), stack(outs_s, 0), stack(outs_s, 1))
```

```python
import functools

import jax
import jax.numpy as jnp
from jax import lax
from jax.experimental import pallas as pl
from jax.experimental.pallas import tpu as pltpu

F32 = jnp.float32
BF16 = jnp.bfloat16
EPS = 1e-6
NEG_INF = -1e30
LOG2E = 1.4426950408889634
LANES = 128
SUBLANES = 8
VMEM_LIMIT_BYTES = 60 * 1024 * 1024
N_TILE = 512
GLU_HALF = N_TILE // 2
FFN_TILE = 512
W_RING = 3


def _params(n_axes, vmem=VMEM_LIMIT_BYTES):
    return pltpu.CompilerParams(dimension_semantics=("arbitrary",) * n_axes,
                                vmem_limit_bytes=vmem)


def _const_spec(shape):
    nd = len(shape)
    return pl.BlockSpec(shape, lambda *_: (0,) * nd, pipeline_mode=pl.Buffered(1))


def _sigmoid(x):
    return jax.nn.sigmoid(x)


def _in_proj_kernel(x_ref, g_ref, w_ref, wf_ref, bf_ref, qg_ref, kg_ref,
                    u_ref, q_ref, kf_ref, vf_ref, ga_ref, gb_ref,
                    lf_ref, lft_ref, h_ref, w_buf, w_sem, *, rc, rc_pre, n_row_tiles, n_glu,
                    n_att, n_gate, n_heads, scale):
    i = pl.program_id(0)
    j = pl.program_id(1)
    tm = x_ref.shape[0]
    n_chunks = tm // rc
    j_q = n_glu
    j_k = j_q + n_att
    j_v = j_k + n_att
    j_ga = j_v + n_att
    j_gb = j_ga + n_gate
    n_tiles = j_gb + n_gate
    heads_per_tile = N_TILE // LANES

    n_total = n_row_tiles * n_tiles
    it = i * n_tiles + j
    slot = it % W_RING

    def weight_copy(step_index):
        tile = step_index % n_tiles
        cols = pl.ds(pl.multiple_of(tile * N_TILE, N_TILE), N_TILE)
        ring_slot = step_index % W_RING
        return pltpu.make_async_copy(w_ref.at[:, cols], w_buf.at[ring_slot], w_sem.at[ring_slot])

    @pl.when(it == 0)
    def _():
        for ahead in range(min(W_RING - 1, n_total)):
            weight_copy(ahead).start()

    @pl.when(it + W_RING - 1 < n_total)
    def _():
        weight_copy(it + W_RING - 1).start()

    weight_copy(it).wait()

    @pl.when(j == 0)
    def _():
        for c in range(tm // rc_pre):
            rows = pl.ds(c * rc_pre, rc_pre)
            x = x_ref[rows, :]
            ms = jnp.mean(x * x, axis=-1, keepdims=True)
            hb = ((x * lax.rsqrt(ms + EPS)) * g_ref[...]).astype(BF16)
            h_ref[rows, :] = hb
            f = jnp.dot(hb, wf_ref[...], preferred_element_type=F32) + bf_ref[...]
            lf = jnp.minimum(f, 0.0) - jnp.log1p(jnp.exp(-jnp.abs(f)))
            lf_ref[rows, :] = lf[:, :n_heads]
            lft_ref[0, :, rows] = lf.T[:n_heads, :]

    def step(epilogue):
        for c in range(n_chunks):
            rows = pl.ds(c * rc, rc)
            acc = jnp.dot(h_ref[rows, :], w_buf[slot], preferred_element_type=F32)
            epilogue(rows, acc)

    def head_norm(acc, gain_ref, hh):
        xh = acc[:, hh * LANES:(hh + 1) * LANES]
        ms = jnp.mean(xh * xh, axis=-1, keepdims=True)
        return (xh * lax.rsqrt(ms + EPS)) * gain_ref[...]

    def ep_glu(rows, acc):
        u_ref[rows, :] = acc[:, :GLU_HALF] * _sigmoid(acc[:, GLU_HALF:])

    def ep_q(rows, acc):
        for hh in range(heads_per_tile):
            qn = head_norm(acc, qg_ref, hh) * scale
            q_ref[rows, hh * LANES:(hh + 1) * LANES] = qn.astype(BF16)

    def ep_k(rows, acc):
        for hh in range(heads_per_tile):
            kf_ref[rows, hh * LANES:(hh + 1) * LANES] = head_norm(acc, kg_ref, hh)

    def ep_v(rows, acc):
        vf_ref[rows, :] = acc

    def ep_ga(rows, acc):
        ga_ref[rows, :] = _sigmoid(acc).astype(BF16)

    def ep_gb(rows, acc):
        gb_ref[rows, :] = _sigmoid(acc).astype(BF16)

    @pl.when(j < j_q)
    def _():
        step(ep_glu)

    @pl.when((j >= j_q) & (j < j_k))
    def _():
        step(ep_q)

    @pl.when((j >= j_k) & (j < j_v))
    def _():
        step(ep_k)

    @pl.when((j >= j_v) & (j < j_ga))
    def _():
        step(ep_v)

    @pl.when((j >= j_ga) & (j < j_gb))
    def _():
        step(ep_ga)

    @pl.when(j >= j_gb)
    def _():
        step(ep_gb)


def _in_proj(x2d, norm_g, w_tiles, wf_pad, bf_pad, q_gain, k_gain, *, n_seq, d_conv, att_w,
             n_heads, tm, rc):
    m, d_model = x2d.shape
    seq = m // n_seq
    tiles_per_seq = seq // tm
    n_glu = d_conv // GLU_HALF
    n_att = att_w // N_TILE
    n_gate = d_model // N_TILE
    n_tiles = n_glu + 3 * n_att + 2 * n_gate
    scale = float(LANES) ** -0.5 * LOG2E
    j_q, j_k, j_v = n_glu, n_glu + n_att, n_glu + 2 * n_att
    j_ga = n_glu + 3 * n_att
    j_gb = j_ga + n_gate

    def col(start, count):
        return lambda i, j: (i, jnp.clip(j - start, 0, count - 1))

    kernel = functools.partial(_in_proj_kernel, rc=rc, rc_pre=min(256, tm), n_row_tiles=m // tm,
                               n_glu=n_glu, n_att=n_att, n_gate=n_gate, n_heads=n_heads,
                               scale=scale)
    out_shape = (
        jax.ShapeDtypeStruct((m, d_conv), F32),
        jax.ShapeDtypeStruct((m, att_w), BF16),
        jax.ShapeDtypeStruct((m, att_w), F32),
        jax.ShapeDtypeStruct((m, att_w), F32),
        jax.ShapeDtypeStruct((m, d_model), BF16),
        jax.ShapeDtypeStruct((m, d_model), BF16),
        jax.ShapeDtypeStruct((m, n_heads), F32),
        jax.ShapeDtypeStruct((n_seq, n_heads, seq), F32),
    )
    out_specs = (
        pl.BlockSpec((tm, GLU_HALF), col(0, n_glu)),
        pl.BlockSpec((tm, N_TILE), col(j_q, n_att)),
        pl.BlockSpec((tm, N_TILE), col(j_k, n_att)),
        pl.BlockSpec((tm, N_TILE), col(j_v, n_att)),
        pl.BlockSpec((tm, N_TILE), col(j_ga, n_gate)),
        pl.BlockSpec((tm, N_TILE), col(j_gb, n_gate)),
        pl.BlockSpec((tm, n_heads), lambda i, j: (i, 0)),
        pl.BlockSpec((1, n_heads, tm), lambda i, j: (i // tiles_per_seq, 0, i % tiles_per_seq)),
    )
    in_specs = [
        pl.BlockSpec((tm, d_model), lambda i, j: (i, 0)),
        _const_spec((1, d_model)),
        pl.BlockSpec(memory_space=pl.ANY),
        _const_spec((d_model, LANES)),
        _const_spec((1, LANES)),
        _const_spec((1, LANES)),
        _const_spec((1, LANES)),
    ]
    return pl.pallas_call(
        kernel,
        grid=(m // tm, n_tiles),
        in_specs=in_specs,
        out_specs=out_specs,
        out_shape=out_shape,
        scratch_shapes=[pltpu.VMEM((tm, d_model), BF16),
                        pltpu.VMEM((W_RING, d_model, N_TILE), BF16),
                        pltpu.SemaphoreType.DMA((W_RING,))],
        compiler_params=_params(2),
        name="in_proj",
    )(x2d, norm_g, w_tiles, wf_pad, bf_pad, q_gain, k_gain)


def _cumsum_kernel(lf_ref, tri_ref, c_ref):
    n_blocks = lf_ref.shape[2] // LANES
    run = jnp.zeros((lf_ref.shape[1], 1), F32)
    for blk in range(n_blocks):
        cols = slice(blk * LANES, (blk + 1) * LANES)
        cb = jnp.dot(lf_ref[0, :, cols], tri_ref[...], precision=lax.Precision.HIGHEST,
                     preferred_element_type=F32) + run
        c_ref[0, :, cols] = cb
        run = cb[:, LANES - 1:LANES]


def _cumsum(lft):
    n_seq, n_heads, length = lft.shape
    idx = jnp.arange(LANES)
    tri = (idx[:, None] <= idx[None, :]).astype(F32)
    return pl.pallas_call(
        _cumsum_kernel,
        grid=(n_seq,),
        in_specs=[pl.BlockSpec((1, n_heads, length), lambda b: (b, 0, 0)),
                  _const_spec((LANES, LANES))],
        out_specs=pl.BlockSpec((1, n_heads, length), lambda b: (b, 0, 0)),
        out_shape=jax.ShapeDtypeStruct(lft.shape, F32),
        compiler_params=_params(1),
        name="logf_cumsum",
    )(lft, tri)


def _conv_kernel(u_ref, ctx_ref, w_ref, b_ref, lg_ref, lb_ref, z_ref, xbuf, ybuf,
                 *, n_taps, row_step, halo, rb_rows):
    i = pl.program_id(1)
    tc, d_conv = u_ref.shape
    first_tap = halo - (n_taps - 1) * row_step

    @pl.when(i == 0)
    def _():
        xbuf[0:halo, :] = ctx_ref[0]

    @pl.when(i > 0)
    def _():
        xbuf[0:halo, :] = xbuf[tc:tc + halo, :]

    xbuf[halo:halo + tc, :] = u_ref[...]

    def lane_chunk(cc, carry):
        cols = pl.ds(pl.multiple_of(cc * LANES, LANES), LANES)
        for rb in range(tc // rb_rows):
            r0 = rb * rb_rows
            acc = jnp.broadcast_to(b_ref[:, cols], (rb_rows, LANES))
            for k in range(n_taps):
                xk = xbuf[pl.ds(r0 + first_tap + k * row_step, rb_rows), cols]
                acc = acc + w_ref[k:k + 1, cols] * xk
            ybuf[pl.ds(r0, rb_rows), cols] = acc
        return carry

    lax.fori_loop(0, d_conv // LANES, lane_chunk, 0)

    for rb in range(tc // rb_rows):
        rows = pl.ds(rb * rb_rows, rb_rows)
        y = ybuf[rows, :]
        mu = jnp.mean(y, axis=-1, keepdims=True)
        yc = y - mu
        var = jnp.mean(yc * yc, axis=-1, keepdims=True)
        zn = (yc * lax.rsqrt(var + EPS)) * lg_ref[...] + lb_ref[...]
        z_ref[rows, :] = (zn * _sigmoid(zn)).astype(BF16)


def _conv(u2d, ctx, w_pad, bias, ln_g, ln_b, *, n_seq, n_taps, row_step, tc, rb_rows):
    m, d_conv = u2d.shape
    halo = ctx.shape[1]
    tiles_per_seq = (m // n_seq) // tc
    kernel = functools.partial(_conv_kernel, n_taps=n_taps, row_step=row_step, halo=halo,
                               rb_rows=rb_rows)
    return pl.pallas_call(
        kernel,
        grid=(n_seq, tiles_per_seq),
        in_specs=[
            pl.BlockSpec((tc, d_conv), lambda b, i: (b * tiles_per_seq + i, 0)),
            pl.BlockSpec((1, halo, d_conv), lambda b, i: (b, 0, 0)),
            _const_spec(w_pad.shape),
            _const_spec((1, d_conv)),
            _const_spec((1, d_conv)),
            _const_spec((1, d_conv)),
        ],
        out_specs=pl.BlockSpec((tc, d_conv), lambda b, i: (b * tiles_per_seq + i, 0)),
        out_shape=jax.ShapeDtypeStruct((m, d_conv), BF16),
        scratch_shapes=[pltpu.VMEM((halo + tc, d_conv), F32), pltpu.VMEM((tc, d_conv), F32)],
        compiler_params=_params(2),
        name="conv_branch",
    )(u2d, ctx, w_pad, bias, ln_g, ln_b)


CONV_BLOCK = 64
X_PITCH = CONV_BLOCK + 4
Y_PITCH = CONV_BLOCK + SUBLANES


def _conv_prompt_kernel(u_ref, ctx_ref, w_ref, b_ref, lg_ref, lb_ref, z_ref, xbuf, ybuf,
                        *, n_taps, r_group):
    i = pl.program_id(1)
    tc, d_conv = u_ref.shape
    n_slabs = d_conv // LANES
    n_blocks = tc // CONV_BLOCK

    @pl.when(i == 0)
    def _():
        for cc in range(n_slabs):
            xbuf[cc, 0:CONV_BLOCK, :] = ctx_ref[0, :, cc * LANES:(cc + 1) * LANES]

    @pl.when(i > 0)
    def _():
        for cc in range(n_slabs):
            xbuf[cc, 0:CONV_BLOCK, :] = xbuf[cc, pl.ds(n_blocks * X_PITCH, CONV_BLOCK), :]

    for cc in range(n_slabs):
        for blk in range(n_blocks):
            xbuf[cc, pl.ds((blk + 1) * X_PITCH, CONV_BLOCK), :] = (
                u_ref[blk * CONV_BLOCK:(blk + 1) * CONV_BLOCK, cc * LANES:(cc + 1) * LANES])

    def slab(cc, carry):
        bias = jnp.broadcast_to(b_ref[cc], (SUBLANES, LANES))
        for r0 in range(0, CONV_BLOCK, r_group):
            accs = [bias] * r_group
            for k in range(n_taps):
                wk = jnp.broadcast_to(w_ref[cc, k:k + 1, :], (SUBLANES, LANES))
                for rr in range(r_group):
                    d = r0 + rr - (n_taps - 1) + k
                    start = X_PITCH + d if d >= 0 else CONV_BLOCK + d
                    xk = xbuf[cc, pl.ds(start, SUBLANES, stride=X_PITCH), :]
                    accs[rr] = accs[rr] + wk * xk
            for rr in range(r_group):
                ybuf[cc, pl.ds(r0 + rr, SUBLANES, stride=Y_PITCH), :] = accs[rr]
        return carry

    lax.fori_loop(0, n_slabs, slab, 0)

    for blk in range(n_blocks):
        y = jnp.concatenate([ybuf[cc, pl.ds(blk * Y_PITCH, CONV_BLOCK), :]
                             for cc in range(n_slabs)], axis=1)
        mu = jnp.mean(y, axis=-1, keepdims=True)
        yc = y - mu
        var = jnp.mean(yc * yc, axis=-1, keepdims=True)
        zn = (yc * lax.rsqrt(var + EPS)) * lg_ref[...] + lb_ref[...]
        z_ref[blk * CONV_BLOCK:(blk + 1) * CONV_BLOCK, :] = (zn * _sigmoid(zn)).astype(BF16)


def _conv_prompt(u2d, ctx, w_slabs, b_slabs, ln_g, ln_b, *, n_seq, n_taps):
    m, d_conv = u2d.shape
    n_slabs = d_conv // LANES
    tc = CONV_BLOCK * SUBLANES
    tiles_per_seq = (m // n_seq) // tc
    kernel = functools.partial(_conv_prompt_kernel, n_taps=n_taps, r_group=SUBLANES)
    return pl.pallas_call(
        kernel,
        grid=(n_seq, tiles_per_seq),
        in_specs=[
            pl.BlockSpec((tc, d_conv), lambda b, i: (b * tiles_per_seq + i, 0)),
            pl.BlockSpec((1, CONV_BLOCK, d_conv), lambda b, i: (b, 0, 0)),
            _const_spec(w_slabs.shape),
            _const_spec(b_slabs.shape),
            _const_spec((1, d_conv)),
            _const_spec((1, d_conv)),
        ],
        out_specs=pl.BlockSpec((tc, d_conv), lambda b, i: (b * tiles_per_seq + i, 0)),
        out_shape=jax.ShapeDtypeStruct((m, d_conv), BF16),
        scratch_shapes=[
            pltpu.VMEM((n_slabs, (SUBLANES + 1) * X_PITCH + 4, LANES), F32),
            pltpu.VMEM((n_slabs, SUBLANES * Y_PITCH, LANES), F32)],
        compiler_params=_params(2),
        name="conv_branch_prompt",
    )(u2d, ctx, w_slabs, b_slabs, ln_g, ln_b)


def _attn_kernel(q_ref, k_ref, v_ref, c_ref, o_ref, *, tq):
    h = pl.program_id(1)
    seq = q_ref.shape[1]
    nq = seq // tq
    c2 = c_ref[0, pl.ds(h, 1), :] * LOG2E
    row = lax.broadcasted_iota(jnp.int32, (tq, tq), 0)
    colk = lax.broadcasted_iota(jnp.int32, (tq, tq), 1)
    causal = colk <= row
    nt = (((1,), (1,)), ((), ()))
    qs = [q_ref[0, i * tq:(i + 1) * tq, :] for i in range(nq)]
    m = [None] * nq
    l = [None] * nq
    acc = [None] * nq
    for j in range(nq):
        keys = slice(j * tq, (j + 1) * tq)
        kj = k_ref[0, keys, :].astype(BF16)
        vj = v_ref[0, keys, :].astype(BF16)
        cj = c2[:, keys]
        for i in range(j, nq):
            s = lax.dot_general(qs[i], kj, nt, preferred_element_type=F32) - cj
            if i == j:
                s = jnp.where(causal, s, NEG_INF)
            s_max = jnp.max(s, axis=1, keepdims=True)
            if j == 0:
                m[i] = s_max
                p = jnp.exp2(s - s_max)
                l[i] = jnp.sum(p, axis=1, keepdims=True)
                acc[i] = jnp.dot(p.astype(BF16), vj, preferred_element_type=F32)
            else:
                m_new = jnp.maximum(m[i], s_max)
                alpha = jnp.exp2(m[i] - m_new)
                p = jnp.exp2(s - m_new)
                l[i] = alpha * l[i] + jnp.sum(p, axis=1, keepdims=True)
                acc[i] = alpha * acc[i] + jnp.dot(p.astype(BF16), vj,
                                                  preferred_element_type=F32)
                m[i] = m_new
        o_ref[0, keys, :] = (acc[j] / l[j]).astype(BF16)


def _attention(q3d, k3d, v3d, c, *, tq):
    n_seq, seq, att_w = k3d.shape
    n_heads = att_w // LANES
    kernel = functools.partial(_attn_kernel, tq=tq)
    head_spec = pl.BlockSpec((1, seq, LANES), lambda b, h: (b, 0, h))
    return pl.pallas_call(
        kernel,
        grid=(n_seq, n_heads),
        in_specs=[head_spec, head_spec, head_spec,
                  pl.BlockSpec((1, n_heads, seq), lambda b, h: (b, 0, 0))],
        out_specs=head_spec,
        out_shape=jax.ShapeDtypeStruct((n_seq, seq, att_w), BF16),
        compiler_params=_params(2),
        name="fox_attention_prompt",
    )(q3d, k3d, v3d, c)


def _attn_sample_kernel(q_ref, kp_ref, vp_ref, kn_ref, vn_ref, c_ref, o_ref, *, past, n_heads):
    t = q_ref.shape[0]
    nt = (((1,), (1,)), ((), ()))
    row = lax.broadcasted_iota(jnp.int32, (t, t), 0)
    colk = lax.broadcasted_iota(jnp.int32, (t, t), 1)
    causal = colk <= row
    for h in range(n_heads):
        lanes = slice(h * LANES, (h + 1) * LANES)
        q = q_ref[:, lanes]
        c2 = c_ref[0, h:h + 1, :] * LOG2E
        k_past = kp_ref[0, pl.ds(h, past, stride=n_heads), :].astype(BF16)
        v_past = vp_ref[0, pl.ds(h, past, stride=n_heads), :].astype(BF16)
        s_p = lax.dot_general(q, k_past, nt, preferred_element_type=F32) - c2[:, :past]
        s_n = lax.dot_general(q, kn_ref[:, lanes].astype(BF16), nt,
                              preferred_element_type=F32) - c2[:, past:past + t]
        s_n = jnp.where(causal, s_n, NEG_INF)
        m = jnp.maximum(jnp.max(s_p, axis=1, keepdims=True), jnp.max(s_n, axis=1, keepdims=True))
        p_p = jnp.exp2(s_p - m)
        p_n = jnp.exp2(s_n - m)
        l = jnp.sum(p_p, axis=1, keepdims=True) + jnp.sum(p_n, axis=1, keepdims=True)
        acc = jnp.dot(p_p.astype(BF16), v_past, preferred_element_type=F32)
        acc = acc + jnp.dot(p_n.astype(BF16), vn_ref[:, lanes].astype(BF16),
                            preferred_element_type=F32)
        o_ref[:, lanes] = (acc / l).astype(BF16)


def _attention_sample(q2d, k_new, v_new, k_past, v_past, c, *, t):
    n_seq, past_rows, _ = k_past.shape
    att_w = q2d.shape[1]
    n_heads = att_w // LANES
    kernel = functools.partial(_attn_sample_kernel, past=past_rows // n_heads, n_heads=n_heads)
    new_spec = pl.BlockSpec((t, att_w), lambda b: (b, 0))
    past_spec = pl.BlockSpec((1, past_rows, LANES), lambda b: (b, 0, 0))
    return pl.pallas_call(
        kernel,
        grid=(n_seq,),
        in_specs=[new_spec, past_spec, past_spec, new_spec, new_spec,
                  pl.BlockSpec((1, n_heads, c.shape[2]), lambda b: (b, 0, 0))],
        out_specs=new_spec,
        out_shape=jax.ShapeDtypeStruct((n_seq * t, att_w), BF16),
        compiler_params=_params(1),
        name="fox_attention_sample",
    )(q2d, k_past, v_past, k_new, v_new, c)


def _mix_kernel(z_ref, o_ref, ga_ref, gb_ref, x_ref, pw_ref, wa_ref, wo_ref, y_ref):
    a_out = jnp.dot(z_ref[...], pw_ref[...], preferred_element_type=F32)
    b_out = jnp.dot(o_ref[...], wa_ref[...], preferred_element_type=F32)
    mix = ga_ref[...].astype(F32) * a_out + gb_ref[...].astype(F32) * b_out
    y_ref[...] = x_ref[...] + jnp.dot(mix.astype(BF16), wo_ref[...], preferred_element_type=F32)


def _mix(z, o, ga, gb, x2d, pw, wa, wo, *, tm):
    m, d_model = x2d.shape

    def row(width):
        return pl.BlockSpec((tm, width), lambda i: (i, 0))

    return pl.pallas_call(
        _mix_kernel,
        grid=(m // tm,),
        in_specs=[row(z.shape[1]), row(o.shape[1]), row(d_model), row(d_model), row(d_model),
                  _const_spec(pw.shape), _const_spec(wa.shape), _const_spec(wo.shape)],
        out_specs=row(d_model),
        out_shape=jax.ShapeDtypeStruct((m, d_model), F32),
        compiler_params=_params(1),
        name="mix_out",
    )(z, o, ga, gb, x2d, pw, wa, wo)


def _ffn_kernel(x_ref, g_ref, wg_ref, wu_ref, wd_ref, cw_ref, cb_ref, st_ref,
                y_ref, nf_ref, h_ref, gbuf, carry, *, tiles_per_seq, row_step, halo, rc, rc_pre):
    i = pl.program_id(0)
    j = pl.program_id(1)
    tm = x_ref.shape[0]
    tf = wg_ref.shape[1]
    n_chunks = tm // rc
    cols = pl.ds(pl.multiple_of(j * tf, tf), tf)

    @pl.when(j == 0)
    def _():
        for c in range(tm // rc_pre):
            rows = pl.ds(c * rc_pre, rc_pre)
            x = x_ref[rows, :]
            ms = jnp.mean(x * x, axis=-1, keepdims=True)
            h_ref[rows, :] = ((x * lax.rsqrt(ms + EPS)) * g_ref[...]).astype(BF16)
            y_ref[rows, :] = x

    first_in_seq = (i % tiles_per_seq) == 0

    @pl.when(first_in_seq)
    def _():
        gbuf[0:halo, :] = st_ref[0]

    @pl.when(jnp.logical_not(first_in_seq))
    def _():
        gbuf[0:halo, :] = carry[:, cols]

    w0 = cw_ref[0:1, :]
    w1 = cw_ref[1:2, :]
    w2 = cw_ref[2:3, :]
    for c in range(n_chunks):
        rows = pl.ds(c * rc, rc)
        hb = h_ref[rows, :]
        g = jnp.dot(hb, wg_ref[...], preferred_element_type=F32)
        up = jnp.dot(hb, wu_ref[...], preferred_element_type=F32)
        gbuf[pl.ds(halo + c * rc, rc), :] = g
        g2 = gbuf[pl.ds(halo + c * rc - 2 * row_step, rc), :]
        g1 = gbuf[pl.ds(halo + c * rc - row_step, rc), :]
        gc = w0 * g2 + w1 * g1 + w2 * g + cb_ref[...]
        act = ((gc * _sigmoid(gc)) * up).astype(BF16)
        y_ref[rows, :] += jnp.dot(act, wd_ref[...], preferred_element_type=F32)

    tail = gbuf[tm:tm + halo, :]
    carry[:, cols] = tail
    nf_ref[0] = tail


def _ffn(x2d, norm_g, w_up, wd, conv_w, conv_b, state, *, n_seq, row_step, tm, tf, rc):
    m, d_model = x2d.shape
    d_ff = wd.shape[0]
    n_ff_tiles = d_ff // tf
    halo = state.shape[1]
    tiles_per_seq = (m // n_seq) // tm
    kernel = functools.partial(_ffn_kernel, tiles_per_seq=tiles_per_seq, row_step=row_step,
                               halo=halo, rc=rc, rc_pre=min(256, tm))
    y, new_state = pl.pallas_call(
        kernel,
        grid=(m // tm, n_ff_tiles),
        in_specs=[
            pl.BlockSpec((tm, d_model), lambda i, j: (i, 0)),
            _const_spec((1, d_model)),
            pl.BlockSpec((d_model, tf), lambda i, j: (0, j)),
            pl.BlockSpec((d_model, tf), lambda i, j: (0, n_ff_tiles + j)),
            pl.BlockSpec((tf, d_model), lambda i, j: (j, 0)),
            pl.BlockSpec((SUBLANES, tf), lambda i, j: (0, j)),
            pl.BlockSpec((1, tf), lambda i, j: (0, j)),
            pl.BlockSpec((1, halo, tf), lambda i, j: (i // tiles_per_seq, 0, j)),
        ],
        out_specs=(
            pl.BlockSpec((tm, d_model), lambda i, j: (i, 0)),
            pl.BlockSpec((1, halo, tf), lambda i, j: (i, 0, j)),
        ),
        out_shape=(jax.ShapeDtypeStruct((m, d_model), F32),
                   jax.ShapeDtypeStruct((m // tm, halo, d_ff), F32)),
        scratch_shapes=[pltpu.VMEM((tm, d_model), BF16),
                        pltpu.VMEM((halo + tm, tf), F32),
                        pltpu.VMEM((halo, d_ff), F32)],
        compiler_params=_params(2),
        name="conv_ffn",
    )(x2d, norm_g, w_up, w_up, wd, conv_w, conv_b, state)
    return y, new_state[tiles_per_seq - 1::tiles_per_seq]


def _prep_weights(norm_mix_g, w_in, b_forget, conv_dw_w, conv_dw_b, conv_ln_g, conv_ln_b,
                  conv_pw_out, q_norm_g, k_norm_g, w_att_out, w_out, norm_ffn_g, w_up,
                  ffn_dw_w, ffn_dw_b, w_down, *, n_heads):
    d_model = w_in.shape[0]
    d_conv = conv_dw_w.shape[1]
    att_w = w_att_out.shape[0]
    d_ff = w_down.shape[0]
    o0 = 2 * d_conv
    o3 = o0 + 3 * att_w
    o4 = o3 + n_heads
    n_glu = d_conv // GLU_HALF
    glu_cols = []
    for t in range(n_glu):
        glu_cols.append(w_in[:, t * GLU_HALF:(t + 1) * GLU_HALF])
        glu_cols.append(w_in[:, d_conv + t * GLU_HALF:d_conv + (t + 1) * GLU_HALF])
    w_tiles = jnp.concatenate(glu_cols + [w_in[:, o0:o3], w_in[:, o4:]], axis=1).astype(BF16)
    wf_pad = jnp.pad(w_in[:, o3:o4], ((0, 0), (0, LANES - n_heads))).astype(BF16)
    bf_pad = jnp.pad(b_forget.reshape(1, n_heads), ((0, 0), (0, LANES - n_heads)))
    n_taps = conv_dw_w.shape[0]
    conv_w = jnp.pad(conv_dw_w, ((0, -n_taps % SUBLANES), (0, 0)))
    n_slabs = d_conv // LANES
    return dict(
        norm_mix_g=norm_mix_g.reshape(1, d_model), w_tiles=w_tiles, wf_pad=wf_pad, bf_pad=bf_pad,
        q_gain=q_norm_g.reshape(1, LANES), k_gain=k_norm_g.reshape(1, LANES),
        conv_w=conv_w,
        conv_w_slabs=conv_w.reshape(-1, n_slabs, LANES).transpose(1, 0, 2),
        conv_b=conv_dw_b.reshape(1, d_conv), conv_b_slabs=conv_dw_b.reshape(n_slabs, 1, LANES),
        ln_g=conv_ln_g.reshape(1, d_conv), ln_b=conv_ln_b.reshape(1, d_conv),
        pw=conv_pw_out.astype(BF16), wa=w_att_out.astype(BF16), wo=w_out.astype(BF16),
        norm_ffn_g=norm_ffn_g.reshape(1, d_model),
        w_up=w_up.astype(BF16), wd=w_down.astype(BF16),
        ffn_w=jnp.pad(ffn_dw_w, ((0, SUBLANES - ffn_dw_w.shape[0]), (0, 0))),
        ffn_b=ffn_dw_b.reshape(1, d_ff), n_taps=n_taps, n_ffn_taps=ffn_dw_w.shape[0],
        d_conv=d_conv, att_w=att_w, d_ff=d_ff,
    )


def _time_major(a, n_seq, t):
    return a.reshape(n_seq, t, -1).transpose(1, 0, 2).reshape(n_seq * t, -1)


def _seq_major(a, n_seq, t):
    return a.reshape(t, n_seq, -1).transpose(1, 0, 2).reshape(n_seq * t, -1)


def _layer_prompt(x, w, *, n_heads):
    n_seq, seq, d_model = x.shape
    m = n_seq * seq
    d_conv, att_w, d_ff = w["d_conv"], w["att_w"], w["d_ff"]
    x2d = x.reshape(m, d_model)
    tm = min(1024, seq)
    u, q, kf, vf, ga, gb, lf, lft = _in_proj(
        x2d, w["norm_mix_g"], w["w_tiles"], w["wf_pad"], w["bf_pad"], w["q_gain"], w["k_gain"],
        n_seq=n_seq, d_conv=d_conv, att_w=att_w, n_heads=n_heads, tm=tm, rc=min(256, tm))
    c = _cumsum(lft)
    ctx = jnp.zeros((n_seq, CONV_BLOCK, d_conv), F32)
    z = _conv_prompt(u, ctx, w["conv_w_slabs"], w["conv_b_slabs"], w["ln_g"], w["ln_b"],
                     n_seq=n_seq, n_taps=w["n_taps"])
    o = _attention(q.reshape(n_seq, seq, att_w), kf.reshape(n_seq, seq, att_w),
                   vf.reshape(n_seq, seq, att_w), c, tq=min(512, seq)).reshape(m, att_w)
    x1 = _mix(z, o, ga, gb, x2d, w["pw"], w["wa"], w["wo"], tm=min(512, seq))
    state = jnp.zeros((n_seq, SUBLANES, d_ff), F32)
    y, new_state = _ffn(x1, w["norm_ffn_g"], w["w_up"], w["wd"], w["ffn_w"], w["ffn_b"],
                        state, n_seq=n_seq, row_step=1, tm=tm, tf=FFN_TILE, rc=tm)
    n_keep = w["n_taps"] - 1
    new_conv = u.reshape(n_seq, seq, d_conv)[:, seq - n_keep:]
    new_ffn = new_state[:, SUBLANES - (w["n_ffn_taps"] - 1):]
    return (y.reshape(n_seq, seq, d_model), new_conv, new_ffn,
            kf.reshape(n_seq, seq, n_heads, LANES), vf.reshape(n_seq, seq, n_heads, LANES),
            lf.reshape(n_seq, seq, n_heads))


def _layer_sample(x, conv_buf, ffn_buf, past_k, past_v, past_logf, w, *, n_heads):
    n_seq, t, d_model = x.shape
    m = n_seq * t
    past = past_k.shape[1]
    d_conv, att_w, d_ff = w["d_conv"], w["att_w"], w["d_ff"]
    x2d = x.reshape(m, d_model)
    u, q, kf, vf, ga, gb, lf, lft = _in_proj(
        x2d, w["norm_mix_g"], w["w_tiles"], w["wf_pad"], w["bf_pad"], w["q_gain"], w["k_gain"],
        n_seq=1, d_conv=d_conv, att_w=att_w, n_heads=n_heads, tm=m, rc=m)
    lft_new = lft.reshape(n_heads, n_seq, t).transpose(1, 0, 2)
    pad = -(past + t) % LANES
    lf_all = jnp.concatenate([past_logf.astype(F32).transpose(0, 2, 1), lft_new,
                              jnp.zeros((n_seq, n_heads, pad), F32)], axis=2)
    c = _cumsum(lf_all)
    n_keep = w["n_taps"] - 1
    ctx = conv_buf.astype(F32).transpose(1, 0, 2).reshape(1, n_keep * n_seq, d_conv)
    z_tm = _conv(_time_major(u, n_seq, t), ctx, w["conv_w"], w["conv_b"], w["ln_g"], w["ln_b"],
                 n_seq=1, n_taps=w["n_taps"], row_step=n_seq, tc=m, rb_rows=64)
    z = _seq_major(z_tm, n_seq, t)
    o = _attention_sample(q, kf, vf, past_k.reshape(n_seq, past * n_heads, LANES),
                          past_v.reshape(n_seq, past * n_heads, LANES), c, t=t)
    x1 = _mix(z, o, ga, gb, x2d, w["pw"], w["wa"], w["wo"], tm=m)
    n_fk = w["n_ffn_taps"] - 1
    state = ffn_buf.astype(F32).transpose(1, 0, 2).reshape(1, n_fk * n_seq, d_ff)
    y_tm, new_state = _ffn(_time_major(x1, n_seq, t), w["norm_ffn_g"], w["w_up"], w["wd"],
                           w["ffn_w"], w["ffn_b"], state, n_seq=1, row_step=n_seq, tm=m,
                           tf=FFN_TILE, rc=m)
    y = _seq_major(y_tm, n_seq, t)
    new_conv = jnp.concatenate([conv_buf.astype(F32), u.reshape(n_seq, t, d_conv)],
                               axis=1)[:, -n_keep:]
    new_ffn = new_state.reshape(n_fk, n_seq, d_ff).transpose(1, 0, 2)
    return (y.reshape(n_seq, t, d_model), new_conv, new_ffn,
            kf.reshape(n_seq, t, n_heads, LANES), vf.reshape(n_seq, t, n_heads, LANES),
            lf.reshape(n_seq, t, n_heads))


def kernel(x_prompt, x_sample, cache_k, cache_v, cache_logf, state_conv, state_ffn, norm_mix_g, w_in, b_forget, conv_dw_w, conv_dw_b, conv_ln_g, conv_ln_b, conv_pw_out, q_norm_g, k_norm_g, w_att_out, w_out, norm_ffn_g, w_up, ffn_dw_w, ffn_dw_b, w_down):
    depth = w_in.shape[0]
    n_heads = cache_k.shape[3]
    assert cache_k.shape[4] == LANES
    xp, xs = x_prompt, x_sample
    outs_p, outs_s = [], []
    for l in range(depth):
        w = _prep_weights(norm_mix_g[l], w_in[l], b_forget[l], conv_dw_w[l], conv_dw_b[l],
                          conv_ln_g[l], conv_ln_b[l], conv_pw_out[l], q_norm_g[l], k_norm_g[l],
                          w_att_out[l], w_out[l], norm_ffn_g[l], w_up[l], ffn_dw_w[l],
                          ffn_dw_b[l], w_down[l], n_heads=n_heads)
        xp, *rest_p = _layer_prompt(xp, w, n_heads=n_heads)
        xs, *rest_s = _layer_sample(xs, state_conv[l], state_ffn[l], cache_k[l], cache_v[l],
                                    cache_logf[l], w, n_heads=n_heads)
        outs_p.append(rest_p)
        outs_s.append(rest_s)

    def stack(outs, idx):
        return jnp.stack([o[idx] for o in outs])

    return (xp, xs,
            stack(outs_p, 2), stack(outs_p, 3), stack(outs_p, 4), stack(outs_p, 0), stack(outs_p, 1),
            stack(outs_s, 2), stack(outs_s, 3), stack(outs_s, 4), stack(outs_s, 0), stack(outs_s, 1))
```

```python
import functools

import jax
import jax.numpy as jnp
from jax import lax
from jax.experimental import pallas as pl
from jax.experimental.pallas import tpu as pltpu

F32 = jnp.float32
BF16 = jnp.bfloat16
EPS = 1e-6
NEG_INF = -1e30
LOG2E = 1.4426950408889634
LANES = 128
SUBLANES = 8
VMEM_LIMIT_BYTES = 60 * 1024 * 1024
N_TILE = 1024
GLU_HALF = N_TILE // 2
FFN_TILE = 512
W_RING = 2


def _params(n_axes, vmem=VMEM_LIMIT_BYTES):
    return pltpu.CompilerParams(dimension_semantics=("arbitrary",) * n_axes,
                                vmem_limit_bytes=vmem)


def _const_spec(shape):
    nd = len(shape)
    return pl.BlockSpec(shape, lambda *_: (0,) * nd, pipeline_mode=pl.Buffered(1))


def _sigmoid(x):
    return jax.nn.sigmoid(x)


def _in_proj_kernel(x_ref, g_ref, w_ref, wf_ref, bf_ref, qg_ref, kg_ref,
                    u_ref, kf_ref, vf_ref, pb_ref, lf_ref, lft_ref, h_ref, w_buf, w_sem,
                    *, rc, n_row_tiles, n_glu, n_att, n_gate, n_heads, scale):
    i = pl.program_id(0)
    j = pl.program_id(1)
    tm = x_ref.shape[0]
    n_chunks = tm // rc
    j_q = n_glu
    j_k = j_q + n_att
    j_v = j_k + n_att
    j_ga = j_v + n_att
    j_gb = j_ga + n_gate
    n_tiles = j_gb + n_gate
    heads_per_tile = N_TILE // LANES

    n_total = n_row_tiles * n_tiles
    it = i * n_tiles + j
    slot = it % W_RING

    def weight_copy(step_index):
        tile = step_index % n_tiles
        cols = pl.ds(pl.multiple_of(tile * N_TILE, N_TILE), N_TILE)
        ring_slot = step_index % W_RING
        return pltpu.make_async_copy(w_ref.at[:, cols], w_buf.at[ring_slot], w_sem.at[ring_slot])

    @pl.when(it == 0)
    def _():
        for ahead in range(min(W_RING - 1, n_total)):
            weight_copy(ahead).start()

    @pl.when(it + W_RING - 1 < n_total)
    def _():
        weight_copy(it + W_RING - 1).start()

    weight_copy(it).wait()

    def normalize_rows(rows):
        x = x_ref[rows, :]
        ms = jnp.mean(x * x, axis=-1, keepdims=True)
        hb = ((x * lax.rsqrt(ms + EPS)) * g_ref[...]).astype(BF16)
        h_ref[rows, :] = hb
        f = jnp.dot(hb, wf_ref[...], preferred_element_type=F32) + bf_ref[...]
        lf = jnp.minimum(f, 0.0) - jnp.log1p(jnp.exp(-jnp.abs(f)))
        lf_ref[rows, :] = lf[:, :n_heads]
        lft_ref[0, :, rows] = lf.T[:n_heads, :]
        return hb

    def step(epilogue):
        for c in range(n_chunks):
            rows = pl.ds(c * rc, rc)
            acc = jnp.dot(h_ref[rows, :], w_buf[slot], preferred_element_type=F32)
            epilogue(rows, acc)

    def head_norm(acc, gain_ref, hh):
        xh = acc[:, hh * LANES:(hh + 1) * LANES]
        ms = jnp.mean(xh * xh, axis=-1, keepdims=True)
        return (xh * lax.rsqrt(ms + EPS)) * gain_ref[...]

    def ep_glu(rows, acc):
        half = pl.ds(pl.multiple_of((j % 2) * GLU_HALF, GLU_HALF), GLU_HALF)
        u_ref[rows, half] = acc[:, :GLU_HALF] * _sigmoid(acc[:, GLU_HALF:])

    def ep_q(rows, acc):
        for hh in range(heads_per_tile):
            qn = head_norm(acc, qg_ref, hh) * scale
            pb_ref[rows, hh * LANES:(hh + 1) * LANES] = qn.astype(BF16)

    def ep_k(rows, acc):
        for hh in range(heads_per_tile):
            kf_ref[rows, hh * LANES:(hh + 1) * LANES] = head_norm(acc, kg_ref, hh)

    def ep_v(rows, acc):
        vf_ref[rows, :] = acc

    def ep_gate(rows, acc):
        pb_ref[rows, :] = _sigmoid(acc).astype(BF16)

    @pl.when(j == 0)
    def _():
        for c in range(n_chunks):
            rows = pl.ds(c * rc, rc)
            hb = normalize_rows(rows)
            ep_glu(rows, jnp.dot(hb, w_buf[slot], preferred_element_type=F32))

    @pl.when((j > 0) & (j < j_q))
    def _():
        step(ep_glu)

    @pl.when((j >= j_q) & (j < j_k))
    def _():
        step(ep_q)

    @pl.when((j >= j_k) & (j < j_v))
    def _():
        step(ep_k)

    @pl.when((j >= j_v) & (j < j_ga))
    def _():
        step(ep_v)

    @pl.when(j >= j_ga)
    def _():
        step(ep_gate)


def _in_proj(x2d, norm_g, w_tiles, wf_pad, bf_pad, q_gain, k_gain, *, n_seq, d_conv, att_w,
             n_heads, tm, rc):
    m, d_model = x2d.shape
    seq = m // n_seq
    tiles_per_seq = seq // tm
    n_glu = d_conv // GLU_HALF
    n_att = att_w // N_TILE
    n_gate = d_model // N_TILE
    n_tiles = n_glu + 3 * n_att + 2 * n_gate
    scale = float(LANES) ** -0.5 * LOG2E
    j_q, j_k, j_v = n_glu, n_glu + n_att, n_glu + 2 * n_att
    j_ga = n_glu + 3 * n_att
    j_gb = j_ga + n_gate

    assert n_glu % 2 == 0
    n_u = n_glu // 2

    def col(start, count, per_block=1):
        return lambda i, j: (i, jnp.clip((j - start) // per_block, 0, count - 1))

    def bf16_block(i, j):
        blk = jnp.where(j < j_q, 2 * n_gate,
                        jnp.where(j < j_k, 2 * n_gate + (j - j_q),
                                  jnp.where(j < j_ga, 2 * n_gate + n_att - 1, j - j_ga)))
        return (i, blk)

    kernel = functools.partial(_in_proj_kernel, rc=rc, n_row_tiles=m // tm,
                               n_glu=n_glu, n_att=n_att, n_gate=n_gate, n_heads=n_heads,
                               scale=scale)
    out_shape = (
        jax.ShapeDtypeStruct((m, d_conv), F32),
        jax.ShapeDtypeStruct((m, att_w), F32),
        jax.ShapeDtypeStruct((m, att_w), F32),
        jax.ShapeDtypeStruct((m, 2 * d_model + att_w), BF16),
        jax.ShapeDtypeStruct((m, n_heads), F32),
        jax.ShapeDtypeStruct((n_seq, n_heads, seq), F32),
    )
    out_specs = (
        pl.BlockSpec((tm, N_TILE), col(0, n_u, per_block=2)),
        pl.BlockSpec((tm, N_TILE), col(j_k, n_att)),
        pl.BlockSpec((tm, N_TILE), col(j_v, n_att)),
        pl.BlockSpec((tm, N_TILE), bf16_block),
        pl.BlockSpec((tm, n_heads), lambda i, j: (i, 0)),
        pl.BlockSpec((1, n_heads, tm), lambda i, j: (i // tiles_per_seq, 0, i % tiles_per_seq)),
    )
    in_specs = [
        pl.BlockSpec((tm, d_model), lambda i, j: (i, 0)),
        _const_spec((1, d_model)),
        pl.BlockSpec(memory_space=pl.ANY),
        _const_spec((d_model, LANES)),
        _const_spec((1, LANES)),
        _const_spec((1, LANES)),
        _const_spec((1, LANES)),
    ]
    return pl.pallas_call(
        kernel,
        grid=(m // tm, n_tiles),
        in_specs=in_specs,
        out_specs=out_specs,
        out_shape=out_shape,
        scratch_shapes=[pltpu.VMEM((tm, d_model), BF16),
                        pltpu.VMEM((W_RING, d_model, N_TILE), BF16),
                        pltpu.SemaphoreType.DMA((W_RING,))],
        compiler_params=_params(2),
        name="in_proj",
    )(x2d, norm_g, w_tiles, wf_pad, bf_pad, q_gain, k_gain)


def _cumsum_kernel(lf_ref, tri_ref, c_ref):
    n_blocks = lf_ref.shape[1] // LANES
    local = [jnp.dot(lf_ref[:, blk * LANES:(blk + 1) * LANES], tri_ref[...],
                     precision=lax.Precision.HIGHEST, preferred_element_type=F32)
             for blk in range(n_blocks)]
    run = jnp.zeros((lf_ref.shape[0], 1), F32)
    for blk in range(n_blocks):
        cb = local[blk] + run
        c_ref[:, blk * LANES:(blk + 1) * LANES] = cb
        run = cb[:, LANES - 1:LANES]


def _cumsum(lft):
    n_seq, n_heads, length = lft.shape
    rows = n_seq * n_heads
    idx = jnp.arange(LANES)
    tri = (idx[:, None] <= idx[None, :]).astype(F32)
    c = pl.pallas_call(
        _cumsum_kernel,
        grid=(1,),
        in_specs=[pl.BlockSpec((rows, length), lambda b: (0, 0)),
                  _const_spec((LANES, LANES))],
        out_specs=pl.BlockSpec((rows, length), lambda b: (0, 0)),
        out_shape=jax.ShapeDtypeStruct((rows, length), F32),
        compiler_params=_params(1),
        name="logf_cumsum",
    )(lft.reshape(rows, length), tri)
    return c.reshape(n_seq, n_heads, length)


def _conv_kernel(u_ref, ctx_ref, w_ref, b_ref, lg_ref, lb_ref, z_ref, xbuf, ybuf,
                 *, n_taps, row_step, halo, rb_rows):
    i = pl.program_id(1)
    tc, d_conv = u_ref.shape
    first_tap = halo - (n_taps - 1) * row_step

    @pl.when(i == 0)
    def _():
        xbuf[0:halo, :] = ctx_ref[0]

    @pl.when(i > 0)
    def _():
        xbuf[0:halo, :] = xbuf[tc:tc + halo, :]

    xbuf[halo:halo + tc, :] = u_ref[...]

    def lane_chunk(cc, carry):
        cols = pl.ds(pl.multiple_of(cc * LANES, LANES), LANES)
        for rb in range(tc // rb_rows):
            r0 = rb * rb_rows
            acc = jnp.broadcast_to(b_ref[:, cols], (rb_rows, LANES))
            for k in range(n_taps):
                xk = xbuf[pl.ds(r0 + first_tap + k * row_step, rb_rows), cols]
                acc = acc + w_ref[k:k + 1, cols] * xk
            ybuf[pl.ds(r0, rb_rows), cols] = acc
        return carry

    lax.fori_loop(0, d_conv // LANES, lane_chunk, 0)

    for rb in range(tc // rb_rows):
        rows = pl.ds(rb * rb_rows, rb_rows)
        y = ybuf[rows, :]
        mu = jnp.mean(y, axis=-1, keepdims=True)
        yc = y - mu
        var = jnp.mean(yc * yc, axis=-1, keepdims=True)
        zn = (yc * lax.rsqrt(var + EPS)) * lg_ref[...] + lb_ref[...]
        z_ref[rows, :] = (zn * _sigmoid(zn)).astype(BF16)


def _conv(u2d, ctx, w_pad, bias, ln_g, ln_b, *, n_seq, n_taps, row_step, tc, rb_rows):
    m, d_conv = u2d.shape
    halo = ctx.shape[1]
    tiles_per_seq = (m // n_seq) // tc
    kernel = functools.partial(_conv_kernel, n_taps=n_taps, row_step=row_step, halo=halo,
                               rb_rows=rb_rows)
    return pl.pallas_call(
        kernel,
        grid=(n_seq, tiles_per_seq),
        in_specs=[
            pl.BlockSpec((tc, d_conv), lambda b, i: (b * tiles_per_seq + i, 0)),
            pl.BlockSpec((1, halo, d_conv), lambda b, i: (b, 0, 0)),
            _const_spec(w_pad.shape),
            _const_spec((1, d_conv)),
            _const_spec((1, d_conv)),
            _const_spec((1, d_conv)),
        ],
        out_specs=pl.BlockSpec((tc, d_conv), lambda b, i: (b * tiles_per_seq + i, 0)),
        out_shape=jax.ShapeDtypeStruct((m, d_conv), BF16),
        scratch_shapes=[pltpu.VMEM((halo + tc, d_conv), F32), pltpu.VMEM((tc, d_conv), F32)],
        compiler_params=_params(2),
        name="conv_branch",
    )(u2d, ctx, w_pad, bias, ln_g, ln_b)


CONV_BLOCK = 64
X_PITCH = CONV_BLOCK + 4
Y_PITCH = CONV_BLOCK + SUBLANES


def _conv_prompt_kernel(u_ref, ctx_ref, w_ref, b_ref, lg_ref, lb_ref, z_ref, xbuf, ybuf,
                        *, n_taps, r_group):
    i = pl.program_id(1)
    tc, d_conv = u_ref.shape
    n_slabs = d_conv // LANES
    n_blocks = tc // CONV_BLOCK

    @pl.when(i == 0)
    def _():
        for cc in range(n_slabs):
            xbuf[cc, 0:CONV_BLOCK, :] = ctx_ref[0, :, cc * LANES:(cc + 1) * LANES]

    @pl.when(i > 0)
    def _():
        for cc in range(n_slabs):
            xbuf[cc, 0:CONV_BLOCK, :] = xbuf[cc, pl.ds(n_blocks * X_PITCH, CONV_BLOCK), :]

    for cc in range(n_slabs):
        for blk in range(n_blocks):
            xbuf[cc, pl.ds((blk + 1) * X_PITCH, CONV_BLOCK), :] = (
                u_ref[blk * CONV_BLOCK:(blk + 1) * CONV_BLOCK, cc * LANES:(cc + 1) * LANES])

    def slab(cc, carry):
        bias = jnp.broadcast_to(b_ref[cc], (SUBLANES, LANES))
        for r0 in range(0, CONV_BLOCK, r_group):
            accs = [bias] * r_group
            for k in range(n_taps):
                wk = jnp.broadcast_to(w_ref[cc, k:k + 1, :], (SUBLANES, LANES))
                for rr in range(r_group):
                    d = r0 + rr - (n_taps - 1) + k
                    start = X_PITCH + d if d >= 0 else CONV_BLOCK + d
                    xk = xbuf[cc, pl.ds(start, SUBLANES, stride=X_PITCH), :]
                    accs[rr] = accs[rr] + wk * xk
            for rr in range(r_group):
                ybuf[cc, pl.ds(r0 + rr, SUBLANES, stride=Y_PITCH), :] = accs[rr]
        return carry

    lax.fori_loop(0, n_slabs, slab, 0)

    for blk in range(n_blocks):
        y = jnp.concatenate([ybuf[cc, pl.ds(blk * Y_PITCH, CONV_BLOCK), :]
                             for cc in range(n_slabs)], axis=1)
        mu = jnp.mean(y, axis=-1, keepdims=True)
        yc = y - mu
        var = jnp.mean(yc * yc, axis=-1, keepdims=True)
        zn = (yc * lax.rsqrt(var + EPS)) * lg_ref[...] + lb_ref[...]
        z_ref[blk * CONV_BLOCK:(blk + 1) * CONV_BLOCK, :] = (zn * _sigmoid(zn)).astype(BF16)


def _conv_prompt(u2d, ctx, w_slabs, b_slabs, ln_g, ln_b, *, n_seq, n_taps):
    m = u2d.shape[0]
    d_conv = ln_g.shape[1]
    n_slabs = d_conv // LANES
    tc = CONV_BLOCK * SUBLANES
    tiles_per_seq = (m // n_seq) // tc
    kernel = functools.partial(_conv_prompt_kernel, n_taps=n_taps, r_group=SUBLANES)
    return pl.pallas_call(
        kernel,
        grid=(n_seq, tiles_per_seq),
        in_specs=[
            pl.BlockSpec((tc, d_conv), lambda b, i: (b * tiles_per_seq + i, 0)),
            pl.BlockSpec((1, CONV_BLOCK, d_conv), lambda b, i: (b, 0, 0)),
            _const_spec(w_slabs.shape),
            _const_spec(b_slabs.shape),
            _const_spec((1, d_conv)),
            _const_spec((1, d_conv)),
        ],
        out_specs=pl.BlockSpec((tc, d_conv), lambda b, i: (b * tiles_per_seq + i, 0)),
        out_shape=jax.ShapeDtypeStruct((m, d_conv), BF16),
        scratch_shapes=[
            pltpu.VMEM((n_slabs, (SUBLANES + 1) * X_PITCH + 4, LANES), F32),
            pltpu.VMEM((n_slabs, SUBLANES * Y_PITCH, LANES), F32)],
        compiler_params=_params(2),
        name="conv_branch_prompt",
    )(u2d, ctx, w_slabs, b_slabs, ln_g, ln_b)


def _attn_kernel(q_ref, k_ref, v_ref, c_ref, o_ref, *, tq):
    h = pl.program_id(1)
    seq = q_ref.shape[1]
    nq = seq // tq
    c2 = c_ref[0, pl.ds(h, 1), :] * LOG2E
    row = lax.broadcasted_iota(jnp.int32, (tq, tq), 0)
    colk = lax.broadcasted_iota(jnp.int32, (tq, tq), 1)
    causal = colk <= row
    nt = (((1,), (1,)), ((), ()))
    qs = [q_ref[0, i * tq:(i + 1) * tq, :] for i in range(nq)]
    m = [None] * nq
    l = [None] * nq
    acc = [None] * nq
    for j in range(nq):
        keys = slice(j * tq, (j + 1) * tq)
        kj = k_ref[0, keys, :].astype(BF16)
        vj = v_ref[0, keys, :].astype(BF16)
        cj = c2[:, keys]
        for i in range(j, nq):
            s = lax.dot_general(qs[i], kj, nt, preferred_element_type=F32) - cj
            if i == j:
                s = jnp.where(causal, s, NEG_INF)
            s_max = jnp.max(s, axis=1, keepdims=True)
            if j == 0:
                m[i] = s_max
                p = jnp.exp2(s - s_max)
                l[i] = jnp.sum(p, axis=1, keepdims=True)
                acc[i] = jnp.dot(p.astype(BF16), vj, preferred_element_type=F32)
            else:
                m_new = jnp.maximum(m[i], s_max)
                alpha = jnp.exp2(m[i] - m_new)
                p = jnp.exp2(s - m_new)
                l[i] = alpha * l[i] + jnp.sum(p, axis=1, keepdims=True)
                acc[i] = alpha * acc[i] + jnp.dot(p.astype(BF16), vj,
                                                  preferred_element_type=F32)
                m[i] = m_new
        o_ref[0, keys, :] = (acc[j] / l[j]).astype(BF16)


def _attention(pb3d, k3d, v3d, c, *, q_col, tq):
    n_seq, n_heads, seq = c.shape

    def head_spec(col):
        first = col // LANES
        return pl.BlockSpec((1, seq, LANES), lambda b, h: (b, 0, first + h))

    kernel = functools.partial(_attn_kernel, tq=tq)
    return pl.pallas_call(
        kernel,
        grid=(n_seq, n_heads),
        in_specs=[head_spec(q_col), head_spec(0), head_spec(0),
                  pl.BlockSpec((1, n_heads, seq), lambda b, h: (b, 0, 0))],
        out_specs=head_spec(0),
        out_shape=jax.ShapeDtypeStruct((n_seq, seq, n_heads * LANES), BF16),
        compiler_params=_params(2),
        name="fox_attention_prompt",
    )(pb3d, k3d, v3d, c)


def _attn_sample_kernel(q_ref, kp_ref, vp_ref, kn_ref, vn_ref, c_ref, o_ref, *, past, n_heads):
    t = q_ref.shape[0]
    nt = (((1,), (1,)), ((), ()))
    row = lax.broadcasted_iota(jnp.int32, (t, t), 0)
    colk = lax.broadcasted_iota(jnp.int32, (t, t), 1)
    causal = colk <= row
    for h in range(n_heads):
        lanes = slice(h * LANES, (h + 1) * LANES)
        q = q_ref[:, lanes]
        c2 = c_ref[0, h:h + 1, :] * LOG2E
        k_past = kp_ref[0, pl.ds(h, past, stride=n_heads), :].astype(BF16)
        v_past = vp_ref[0, pl.ds(h, past, stride=n_heads), :].astype(BF16)
        s_p = lax.dot_general(q, k_past, nt, preferred_element_type=F32) - c2[:, :past]
        s_n = lax.dot_general(q, kn_ref[:, lanes].astype(BF16), nt,
                              preferred_element_type=F32) - c2[:, past:past + t]
        s_n = jnp.where(causal, s_n, NEG_INF)
        m = jnp.maximum(jnp.max(s_p, axis=1, keepdims=True), jnp.max(s_n, axis=1, keepdims=True))
        p_p = jnp.exp2(s_p - m)
        p_n = jnp.exp2(s_n - m)
        l = jnp.sum(p_p, axis=1, keepdims=True) + jnp.sum(p_n, axis=1, keepdims=True)
        acc = jnp.dot(p_p.astype(BF16), v_past, preferred_element_type=F32)
        acc = acc + jnp.dot(p_n.astype(BF16), vn_ref[:, lanes].astype(BF16),
                            preferred_element_type=F32)
        o_ref[:, lanes] = (acc / l).astype(BF16)


def _attention_sample(pb2d, k_new, v_new, k_past, v_past, c, *, q_col, t):
    n_seq, past_rows, _ = k_past.shape
    n_heads = c.shape[1]
    att_w = n_heads * LANES
    kernel = functools.partial(_attn_sample_kernel, past=past_rows // n_heads, n_heads=n_heads)

    def new_spec(col):
        blk = col // att_w
        return pl.BlockSpec((t, att_w), lambda b: (b, blk))

    past_spec = pl.BlockSpec((1, past_rows, LANES), lambda b: (b, 0, 0))
    return pl.pallas_call(
        kernel,
        grid=(n_seq,),
        in_specs=[new_spec(q_col), past_spec, past_spec, new_spec(0), new_spec(0),
                  pl.BlockSpec((1, n_heads, c.shape[2]), lambda b: (b, 0, 0))],
        out_specs=new_spec(0),
        out_shape=jax.ShapeDtypeStruct((n_seq * t, att_w), BF16),
        compiler_params=_params(1),
        name="fox_attention_sample",
    )(pb2d, k_past, v_past, k_new, v_new, c)


def _mix_kernel(z_ref, o_ref, ga_ref, gb_ref, x_ref, pw_ref, wa_ref, wo_ref, y_ref):
    a_out = jnp.dot(z_ref[...], pw_ref[...], preferred_element_type=F32)
    b_out = jnp.dot(o_ref[...], wa_ref[...], preferred_element_type=F32)
    mix = ga_ref[...].astype(F32) * a_out + gb_ref[...].astype(F32) * b_out
    y_ref[...] = x_ref[...] + jnp.dot(mix.astype(BF16), wo_ref[...], preferred_element_type=F32)


def _mix(z, o, gates, x2d, pw, wa, wo, *, tm):
    m, d_model = x2d.shape

    def row(width, blk=0):
        return pl.BlockSpec((tm, width), lambda i: (i, blk))

    return pl.pallas_call(
        _mix_kernel,
        grid=(m // tm,),
        in_specs=[row(z.shape[1]), row(o.shape[1]), row(d_model, 0), row(d_model, 1),
                  row(d_model), _const_spec(pw.shape), _const_spec(wa.shape),
                  _const_spec(wo.shape)],
        out_specs=row(d_model),
        out_shape=jax.ShapeDtypeStruct((m, d_model), F32),
        compiler_params=_params(1),
        name="mix_out",
    )(z, o, gates, gates, x2d, pw, wa, wo)


def _ffn_kernel(x_ref, g_ref, wup_hbm, wd_hbm, cw_ref, cb_ref, st_ref,
                y_ref, nf_ref, h_ref, gbuf, carry, wg_buf, wu_buf, wd_buf, sem,
                *, n_row_tiles, tiles_per_seq, row_step, halo, rc, rc_pre, tf):
    i = pl.program_id(0)
    tm = x_ref.shape[0]
    d_ff = wd_hbm.shape[0]
    n_tiles = d_ff // tf
    n_chunks = tm // rc
    n_total = n_row_tiles * n_tiles

    def weight_copies(t, slot):
        c0 = pl.multiple_of(t * tf, tf)
        return (
            pltpu.make_async_copy(wup_hbm.at[:, pl.ds(c0, tf)], wg_buf.at[slot], sem.at[0, slot]),
            pltpu.make_async_copy(wup_hbm.at[:, pl.ds(d_ff + c0, tf)], wu_buf.at[slot],
                                  sem.at[1, slot]),
            pltpu.make_async_copy(wd_hbm.at[pl.ds(c0, tf), :], wd_buf.at[slot], sem.at[2, slot]),
        )

    @pl.when(i == 0)
    def _():
        for cp in weight_copies(0, 0):
            cp.start()

    for c in range(tm // rc_pre):
        rows = pl.ds(c * rc_pre, rc_pre)
        x = x_ref[rows, :]
        ms = jnp.mean(x * x, axis=-1, keepdims=True)
        h_ref[rows, :] = ((x * lax.rsqrt(ms + EPS)) * g_ref[...]).astype(BF16)
        y_ref[rows, :] = x

    first_in_seq = (i % tiles_per_seq) == 0

    def column_tile(t, loop_carry):
        it = i * n_tiles + t
        slot = it % 2
        for cp in weight_copies(t, slot):
            cp.wait()

        @pl.when(it + 1 < n_total)
        def _():
            t_next = jnp.where(t + 1 == n_tiles, 0, t + 1)
            for cp in weight_copies(t_next, 1 - slot):
                cp.start()

        cols = pl.ds(pl.multiple_of(t * tf, tf), tf)

        @pl.when(first_in_seq)
        def _():
            gbuf[0:halo, :] = st_ref[0, :, cols]

        @pl.when(jnp.logical_not(first_in_seq))
        def _():
            gbuf[0:halo, :] = carry[:, cols]

        w0 = cw_ref[0:1, cols]
        w1 = cw_ref[1:2, cols]
        w2 = cw_ref[2:3, cols]
        bias = cb_ref[:, cols]
        for c in range(n_chunks):
            rows = pl.ds(c * rc, rc)
            hb = h_ref[rows, :]
            g = jnp.dot(hb, wg_buf[slot], preferred_element_type=F32)
            up = jnp.dot(hb, wu_buf[slot], preferred_element_type=F32)
            gbuf[pl.ds(halo + c * rc, rc), :] = g
            g2 = gbuf[pl.ds(halo + c * rc - 2 * row_step, rc), :]
            g1 = gbuf[pl.ds(halo + c * rc - row_step, rc), :]
            gc = w0 * g2 + w1 * g1 + w2 * g + bias
            act = ((gc * _sigmoid(gc)) * up).astype(BF16)
            y_ref[rows, :] += jnp.dot(act, wd_buf[slot], preferred_element_type=F32)

        tail = gbuf[tm:tm + halo, :]
        carry[:, cols] = tail
        nf_ref[0, :, cols] = tail
        return loop_carry

    lax.fori_loop(0, n_tiles, column_tile, 0)


def _ffn(x2d, norm_g, w_up, wd, conv_w, conv_b, state, *, n_seq, row_step, tm, tf, rc):
    m, d_model = x2d.shape
    d_ff = wd.shape[0]
    n_ff_tiles = d_ff // tf
    halo = state.shape[1]
    n_row_tiles = m // tm
    tiles_per_seq = (m // n_seq) // tm
    kernel = functools.partial(_ffn_kernel, n_row_tiles=n_row_tiles, tiles_per_seq=tiles_per_seq,
                               row_step=row_step, halo=halo, rc=rc, rc_pre=min(256, tm), tf=tf)
    y, new_state = pl.pallas_call(
        kernel,
        grid=(n_row_tiles,),
        in_specs=[
            pl.BlockSpec((tm, d_model), lambda i: (i, 0)),
            _const_spec((1, d_model)),
            pl.BlockSpec(memory_space=pl.ANY),
            pl.BlockSpec(memory_space=pl.ANY),
            _const_spec((SUBLANES, d_ff)),
            _const_spec((1, d_ff)),
            pl.BlockSpec((1, halo, d_ff), lambda i: (i // tiles_per_seq, 0, 0)),
        ],
        out_specs=(
            pl.BlockSpec((tm, d_model), lambda i: (i, 0)),
            pl.BlockSpec((1, halo, d_ff), lambda i: (i, 0, 0)),
        ),
        out_shape=(jax.ShapeDtypeStruct((m, d_model), F32),
                   jax.ShapeDtypeStruct((n_row_tiles, halo, d_ff), F32)),
        scratch_shapes=[pltpu.VMEM((tm, d_model), BF16),
                        pltpu.VMEM((halo + tm, tf), F32),
                        pltpu.VMEM((halo, d_ff), F32),
                        pltpu.VMEM((2, d_model, tf), BF16),
                        pltpu.VMEM((2, d_model, tf), BF16),
                        pltpu.VMEM((2, tf, d_model), BF16),
                        pltpu.SemaphoreType.DMA((3, 2))],
        compiler_params=_params(1),
        name="conv_ffn",
    )(x2d, norm_g, w_up, wd, conv_w, conv_b, state)
    return y, new_state[tiles_per_seq - 1::tiles_per_seq]


def _prep_weights(norm_mix_g, w_in, b_forget, conv_dw_w, conv_dw_b, conv_ln_g, conv_ln_b,
                  conv_pw_out, q_norm_g, k_norm_g, w_att_out, w_out, norm_ffn_g, w_up,
                  ffn_dw_w, ffn_dw_b, w_down, *, n_heads):
    d_model = w_in.shape[0]
    d_conv = conv_dw_w.shape[1]
    att_w = w_att_out.shape[0]
    d_ff = w_down.shape[0]
    o0 = 2 * d_conv
    o3 = o0 + 3 * att_w
    o4 = o3 + n_heads
    n_glu = d_conv // GLU_HALF
    glu_cols = []
    for t in range(n_glu):
        glu_cols.append(w_in[:, t * GLU_HALF:(t + 1) * GLU_HALF])
        glu_cols.append(w_in[:, d_conv + t * GLU_HALF:d_conv + (t + 1) * GLU_HALF])
    w_tiles = jnp.concatenate(glu_cols + [w_in[:, o0:o3], w_in[:, o4:]], axis=1).astype(BF16)
    wf_pad = jnp.pad(w_in[:, o3:o4], ((0, 0), (0, LANES - n_heads))).astype(BF16)
    bf_pad = jnp.pad(b_forget.reshape(1, n_heads), ((0, 0), (0, LANES - n_heads)))
    n_taps = conv_dw_w.shape[0]
    conv_w = jnp.pad(conv_dw_w, ((0, -n_taps % SUBLANES), (0, 0)))
    n_slabs = d_conv // LANES
    return dict(
        norm_mix_g=norm_mix_g.reshape(1, d_model), w_tiles=w_tiles, wf_pad=wf_pad, bf_pad=bf_pad,
        q_gain=q_norm_g.reshape(1, LANES), k_gain=k_norm_g.reshape(1, LANES),
        conv_w=conv_w,
        conv_w_slabs=conv_w.reshape(-1, n_slabs, LANES).transpose(1, 0, 2),
        conv_b=conv_dw_b.reshape(1, d_conv), conv_b_slabs=conv_dw_b.reshape(n_slabs, 1, LANES),
        ln_g=conv_ln_g.reshape(1, d_conv), ln_b=conv_ln_b.reshape(1, d_conv),
        pw=conv_pw_out.astype(BF16), wa=w_att_out.astype(BF16), wo=w_out.astype(BF16),
        norm_ffn_g=norm_ffn_g.reshape(1, d_model),
        w_up=w_up.astype(BF16), wd=w_down.astype(BF16),
        ffn_w=jnp.pad(ffn_dw_w, ((0, SUBLANES - ffn_dw_w.shape[0]), (0, 0))),
        ffn_b=ffn_dw_b.reshape(1, d_ff), n_taps=n_taps, n_ffn_taps=ffn_dw_w.shape[0],
        d_conv=d_conv, att_w=att_w, d_ff=d_ff,
    )


def _time_major(a, n_seq, t):
    return a.reshape(n_seq, t, -1).transpose(1, 0, 2).reshape(n_seq * t, -1)


def _seq_major(a, n_seq, t):
    return a.reshape(t, n_seq, -1).transpose(1, 0, 2).reshape(n_seq * t, -1)


def _layer_prompt(x, w, *, n_heads):
    n_seq, seq, d_model = x.shape
    m = n_seq * seq
    d_conv, att_w, d_ff = w["d_conv"], w["att_w"], w["d_ff"]
    x2d = x.reshape(m, d_model)
    tm = min(1024, seq)
    u, kf, vf, pb, lf, lft = _in_proj(
        x2d, w["norm_mix_g"], w["w_tiles"], w["wf_pad"], w["bf_pad"], w["q_gain"], w["k_gain"],
        n_seq=n_seq, d_conv=d_conv, att_w=att_w, n_heads=n_heads, tm=tm, rc=min(256, tm))
    q_col = 2 * d_model
    c = _cumsum(lft)
    ctx = jnp.zeros((n_seq, CONV_BLOCK, d_conv), F32)
    z = _conv_prompt(u, ctx, w["conv_w_slabs"], w["conv_b_slabs"], w["ln_g"], w["ln_b"],
                     n_seq=n_seq, n_taps=w["n_taps"])
    o = _attention(pb.reshape(n_seq, seq, -1), kf.reshape(n_seq, seq, att_w),
                   vf.reshape(n_seq, seq, att_w), c, q_col=q_col,
                   tq=min(512, seq)).reshape(m, att_w)
    x1 = _mix(z, o, pb, x2d, w["pw"], w["wa"], w["wo"], tm=min(512, seq))
    state = jnp.zeros((n_seq, SUBLANES, d_ff), F32)
    y, new_state = _ffn(x1, w["norm_ffn_g"], w["w_up"], w["wd"], w["ffn_w"], w["ffn_b"],
                        state, n_seq=n_seq, row_step=1, tm=tm, tf=FFN_TILE, rc=tm)
    n_keep = w["n_taps"] - 1
    new_conv = u.reshape(n_seq, seq, d_conv)[:, seq - n_keep:]
    new_ffn = new_state[:, SUBLANES - (w["n_ffn_taps"] - 1):]
    return (y.reshape(n_seq, seq, d_model), new_conv, new_ffn,
            kf.reshape(n_seq, seq, n_heads, LANES), vf.reshape(n_seq, seq, n_heads, LANES),
            lf.reshape(n_seq, seq, n_heads))


def _layer_sample(x, conv_buf, ffn_buf, past_k, past_v, past_logf, w, *, n_heads):
    n_seq, t, d_model = x.shape
    m = n_seq * t
    past = past_k.shape[1]
    d_conv, att_w, d_ff = w["d_conv"], w["att_w"], w["d_ff"]
    x2d = x.reshape(m, d_model)
    u, kf, vf, pb, lf, lft = _in_proj(
        x2d, w["norm_mix_g"], w["w_tiles"], w["wf_pad"], w["bf_pad"], w["q_gain"], w["k_gain"],
        n_seq=1, d_conv=d_conv, att_w=att_w, n_heads=n_heads, tm=m, rc=m)
    q_col = 2 * d_model
    lft_new = lft.reshape(n_heads, n_seq, t).transpose(1, 0, 2)
    pad = -(past + t) % LANES
    lf_all = jnp.concatenate([past_logf.astype(F32).transpose(0, 2, 1), lft_new,
                              jnp.zeros((n_seq, n_heads, pad), F32)], axis=2)
    c = _cumsum(lf_all)
    n_keep = w["n_taps"] - 1
    ctx = conv_buf.astype(F32).transpose(1, 0, 2).reshape(1, n_keep * n_seq, d_conv)
    z_tm = _conv(_time_major(u, n_seq, t), ctx, w["conv_w"], w["conv_b"], w["ln_g"], w["ln_b"],
                 n_seq=1, n_taps=w["n_taps"], row_step=n_seq, tc=m, rb_rows=64)
    z = _seq_major(z_tm, n_seq, t)
    o = _attention_sample(pb, kf, vf, past_k.reshape(n_seq, past * n_heads, LANES),
                          past_v.reshape(n_seq, past * n_heads, LANES), c, q_col=q_col, t=t)
    x1 = _mix(z, o, pb, x2d, w["pw"], w["wa"], w["wo"], tm=m)
    n_fk = w["n_ffn_taps"] - 1
    state = ffn_buf.astype(F32).transpose(1, 0, 2).reshape(1, n_fk * n_seq, d_ff)
    y_tm, new_state = _ffn(_time_major(x1, n_seq, t), w["norm_ffn_g"], w["w_up"], w["wd"],
                           w["ffn_w"], w["ffn_b"], state, n_seq=1, row_step=n_seq, tm=m,
                           tf=FFN_TILE, rc=m)
    y = _seq_major(y_tm, n_seq, t)
    new_conv = jnp.concatenate([conv_buf.astype(F32), u.reshape(n_seq, t, d_conv)],
                               axis=1)[:, -n_keep:]
    new_ffn = new_state.reshape(n_fk, n_seq, d_ff).transpose(1, 0, 2)
    return (y.reshape(n_seq, t, d_model), new_conv, new_ffn,
            kf.reshape(n_seq, t, n_heads, LANES), vf.reshape(n_seq, t, n_heads, LANES),
            lf.reshape(n_seq, t, n_heads))


def kernel(x_prompt, x_sample, cache_k, cache_v, cache_logf, state_conv, state_ffn, norm_mix_g, w_in, b_forget, conv_dw_w, conv_dw_b, conv_ln_g, conv_ln_b, conv_pw_out, q_norm_g, k_norm_g, w_att_out, w_out, norm_ffn_g, w_up, ffn_dw_w, ffn_dw_b, w_down):
    depth = w_in.shape[0]
    n_heads = cache_k.shape[3]
    assert cache_k.shape[4] == LANES
    xp, xs = x_prompt, x_sample
    outs_p, outs_s = [], []
    for l in range(depth):
        w = _prep_weights(norm_mix_g[l], w_in[l], b_forget[l], conv_dw_w[l], conv_dw_b[l],
                          conv_ln_g[l], conv_ln_b[l], conv_pw_out[l], q_norm_g[l], k_norm_g[l],
                          w_att_out[l], w_out[l], norm_ffn_g[l], w_up[l], ffn_dw_w[l],
                          ffn_dw_b[l], w_down[l], n_heads=n_heads)
        xp, *rest_p = _layer_prompt(xp, w, n_heads=n_heads)
        xs, *rest_s = _layer_sample(xs, state_conv[l], state_ffn[l], cache_k[l], cache_v[l],
                                    cache_logf[l], w, n_heads=n_heads)
        outs_p.append(rest_p)
        outs_s.append(rest_s)

    def stack(outs, idx):
        return jnp.stack([o[idx] for o in outs])

    return (xp, xs,
            stack(outs_p, 2), stack(outs_p, 3), stack(outs_p, 4), stack(outs_p, 0), stack(outs_p, 1),
            stack(outs_s, 2), stack(outs_s, 3), stack(outs_s, 4), stack(outs_s, 0), stack(outs_s, 1))
```

```python
import functools

import jax
import jax.numpy as jnp
from jax import lax
from jax.experimental import pallas as pl
from jax.experimental.pallas import tpu as pltpu

F32 = jnp.float32
BF16 = jnp.bfloat16
EPS = 1e-6
NEG_INF = -1e30
LOG2E = 1.4426950408889634
LANES = 128
SUBLANES = 8
VMEM_LIMIT_BYTES = 60 * 1024 * 1024
N_TILE = 1024
GLU_HALF = N_TILE // 2
FFN_TILE = 512
W_RING = 2


def _params(n_axes, vmem=VMEM_LIMIT_BYTES):
    return pltpu.CompilerParams(dimension_semantics=("arbitrary",) * n_axes,
                                vmem_limit_bytes=vmem)


def _const_spec(shape):
    nd = len(shape)
    return pl.BlockSpec(shape, lambda *_: (0,) * nd, pipeline_mode=pl.Buffered(1))


def _sigmoid(x):
    return jax.nn.sigmoid(x)


def _in_proj_kernel(x_ref, g_ref, w_ref, wf_ref, bf_ref, qg_ref, kg_ref,
                    u_ref, kf_ref, vf_ref, pb_ref, lf_ref, lft_ref, h_ref, w_buf, w_sem,
                    *, rc, n_row_tiles, n_glu, n_att, n_gate, n_heads, scale):
    i = pl.program_id(0)
    j = pl.program_id(1)
    tm = x_ref.shape[0]
    n_chunks = tm // rc
    j_q = n_glu
    j_k = j_q + n_att
    j_v = j_k + n_att
    j_ga = j_v + n_att
    j_gb = j_ga + n_gate
    n_tiles = j_gb + n_gate
    heads_per_tile = N_TILE // LANES

    n_total = n_row_tiles * n_tiles
    it = i * n_tiles + j
    slot = it % W_RING

    def weight_copy(step_index):
        tile = step_index % n_tiles
        cols = pl.ds(pl.multiple_of(tile * N_TILE, N_TILE), N_TILE)
        ring_slot = step_index % W_RING
        return pltpu.make_async_copy(w_ref.at[:, cols], w_buf.at[ring_slot], w_sem.at[ring_slot])

    @pl.when(it == 0)
    def _():
        for ahead in range(min(W_RING - 1, n_total)):
            weight_copy(ahead).start()

    @pl.when(it + W_RING - 1 < n_total)
    def _():
        weight_copy(it + W_RING - 1).start()

    weight_copy(it).wait()

    def normalize_rows(rows):
        x = x_ref[rows, :]
        ms = jnp.mean(x * x, axis=-1, keepdims=True)
        hb = ((x * lax.rsqrt(ms + EPS)) * g_ref[...]).astype(BF16)
        h_ref[rows, :] = hb
        f = jnp.dot(hb, wf_ref[...], preferred_element_type=F32) + bf_ref[...]
        lf = jnp.minimum(f, 0.0) - jnp.log1p(jnp.exp(-jnp.abs(f)))
        lf_ref[rows, :] = lf[:, :n_heads]
        lft_ref[0, :, rows] = lf.T[:n_heads, :]
        return hb

    def step(epilogue):
        for c in range(n_chunks):
            rows = pl.ds(c * rc, rc)
            acc = jnp.dot(h_ref[rows, :], w_buf[slot], preferred_element_type=F32)
            epilogue(rows, acc)

    def head_norm(acc, gain_ref, hh):
        xh = acc[:, hh * LANES:(hh + 1) * LANES]
        ms = jnp.mean(xh * xh, axis=-1, keepdims=True)
        return (xh * lax.rsqrt(ms + EPS)) * gain_ref[...]

    def ep_glu(rows, acc):
        half = pl.ds(pl.multiple_of((j % 2) * GLU_HALF, GLU_HALF), GLU_HALF)
        u_ref[rows, half] = acc[:, :GLU_HALF] * _sigmoid(acc[:, GLU_HALF:])

    def ep_q(rows, acc):
        for hh in range(heads_per_tile):
            qn = head_norm(acc, qg_ref, hh) * scale
            pb_ref[rows, hh * LANES:(hh + 1) * LANES] = qn.astype(BF16)

    def ep_k(rows, acc):
        for hh in range(heads_per_tile):
            kf_ref[rows, hh * LANES:(hh + 1) * LANES] = head_norm(acc, kg_ref, hh)

    def ep_v(rows, acc):
        vf_ref[rows, :] = acc

    def ep_gate(rows, acc):
        pb_ref[rows, :] = _sigmoid(acc).astype(BF16)

    @pl.when(j == 0)
    def _():
        for c in range(n_chunks):
            rows = pl.ds(c * rc, rc)
            hb = normalize_rows(rows)
            ep_glu(rows, jnp.dot(hb, w_buf[slot], preferred_element_type=F32))

    @pl.when((j > 0) & (j < j_q))
    def _():
        step(ep_glu)

    @pl.when((j >= j_q) & (j < j_k))
    def _():
        step(ep_q)

    @pl.when((j >= j_k) & (j < j_v))
    def _():
        step(ep_k)

    @pl.when((j >= j_v) & (j < j_ga))
    def _():
        step(ep_v)

    @pl.when(j >= j_ga)
    def _():
        step(ep_gate)


def _in_proj(x2d, norm_g, w_tiles, wf_pad, bf_pad, q_gain, k_gain, *, n_seq, d_conv, att_w,
             n_heads, tm, rc):
    m, d_model = x2d.shape
    seq = m // n_seq
    tiles_per_seq = seq // tm
    n_glu = d_conv // GLU_HALF
    n_att = att_w // N_TILE
    n_gate = d_model // N_TILE
    n_tiles = n_glu + 3 * n_att + 2 * n_gate
    scale = float(LANES) ** -0.5 * LOG2E
    j_q, j_k, j_v = n_glu, n_glu + n_att, n_glu + 2 * n_att
    j_ga = n_glu + 3 * n_att
    j_gb = j_ga + n_gate

    assert n_glu % 2 == 0
    n_u = n_glu // 2

    def col(start, count, per_block=1):
        return lambda i, j: (i, jnp.clip((j - start) // per_block, 0, count - 1))

    def bf16_block(i, j):
        blk = jnp.where(j < j_q, 2 * n_gate,
                        jnp.where(j < j_k, 2 * n_gate + (j - j_q),
                                  jnp.where(j < j_ga, 2 * n_gate + n_att - 1, j - j_ga)))
        return (i, blk)

    kernel = functools.partial(_in_proj_kernel, rc=rc, n_row_tiles=m // tm,
                               n_glu=n_glu, n_att=n_att, n_gate=n_gate, n_heads=n_heads,
                               scale=scale)
    out_shape = (
        jax.ShapeDtypeStruct((m, d_conv), F32),
        jax.ShapeDtypeStruct((m, att_w), F32),
        jax.ShapeDtypeStruct((m, att_w), F32),
        jax.ShapeDtypeStruct((m, 2 * d_model + att_w), BF16),
        jax.ShapeDtypeStruct((m, n_heads), F32),
        jax.ShapeDtypeStruct((n_seq, n_heads, seq), F32),
    )
    out_specs = (
        pl.BlockSpec((tm, N_TILE), col(0, n_u, per_block=2)),
        pl.BlockSpec((tm, N_TILE), col(j_k, n_att)),
        pl.BlockSpec((tm, N_TILE), col(j_v, n_att)),
        pl.BlockSpec((tm, N_TILE), bf16_block),
        pl.BlockSpec((tm, n_heads), lambda i, j: (i, 0)),
        pl.BlockSpec((1, n_heads, tm), lambda i, j: (i // tiles_per_seq, 0, i % tiles_per_seq)),
    )
    in_specs = [
        pl.BlockSpec((tm, d_model), lambda i, j: (i, 0)),
        _const_spec((1, d_model)),
        pl.BlockSpec(memory_space=pl.ANY),
        _const_spec((d_model, LANES)),
        _const_spec((1, LANES)),
        _const_spec((1, LANES)),
        _const_spec((1, LANES)),
    ]
    return pl.pallas_call(
        kernel,
        grid=(m // tm, n_tiles),
        in_specs=in_specs,
        out_specs=out_specs,
        out_shape=out_shape,
        scratch_shapes=[pltpu.VMEM((tm, d_model), BF16),
                        pltpu.VMEM((W_RING, d_model, N_TILE), BF16),
                        pltpu.SemaphoreType.DMA((W_RING,))],
        compiler_params=_params(2),
        name="in_proj",
    )(x2d, norm_g, w_tiles, wf_pad, bf_pad, q_gain, k_gain)


def _cumsum_kernel(lf_ref, tri_ref, c_ref):
    n_blocks = lf_ref.shape[1] // LANES
    local = [jnp.dot(lf_ref[:, blk * LANES:(blk + 1) * LANES], tri_ref[...],
                     precision=lax.Precision.HIGHEST, preferred_element_type=F32)
             for blk in range(n_blocks)]
    run = jnp.zeros((lf_ref.shape[0], 1), F32)
    for blk in range(n_blocks):
        cb = local[blk] + run
        c_ref[:, blk * LANES:(blk + 1) * LANES] = cb
        run = cb[:, LANES - 1:LANES]


def _cumsum(lft):
    n_seq, n_heads, length = lft.shape
    rows = n_seq * n_heads
    idx = jnp.arange(LANES)
    tri = (idx[:, None] <= idx[None, :]).astype(F32)
    c = pl.pallas_call(
        _cumsum_kernel,
        grid=(1,),
        in_specs=[pl.BlockSpec((rows, length), lambda b: (0, 0)),
                  _const_spec((LANES, LANES))],
        out_specs=pl.BlockSpec((rows, length), lambda b: (0, 0)),
        out_shape=jax.ShapeDtypeStruct((rows, length), F32),
        compiler_params=_params(1),
        name="logf_cumsum",
    )(lft.reshape(rows, length), tri)
    return c.reshape(n_seq, n_heads, length)


def _conv_kernel(u_ref, ctx_ref, w_ref, b_ref, lg_ref, lb_ref, z_ref, xbuf, ybuf,
                 *, n_taps, row_step, halo, rb_rows):
    i = pl.program_id(1)
    tc, d_conv = u_ref.shape
    first_tap = halo - (n_taps - 1) * row_step

    @pl.when(i == 0)
    def _():
        xbuf[0:halo, :] = ctx_ref[0]

    @pl.when(i > 0)
    def _():
        xbuf[0:halo, :] = xbuf[tc:tc + halo, :]

    xbuf[halo:halo + tc, :] = u_ref[...]

    def lane_chunk(cc, carry):
        cols = pl.ds(pl.multiple_of(cc * LANES, LANES), LANES)
        for rb in range(tc // rb_rows):
            r0 = rb * rb_rows
            acc = jnp.broadcast_to(b_ref[:, cols], (rb_rows, LANES))
            for k in range(n_taps):
                xk = xbuf[pl.ds(r0 + first_tap + k * row_step, rb_rows), cols]
                acc = acc + w_ref[k:k + 1, cols] * xk
            ybuf[pl.ds(r0, rb_rows), cols] = acc
        return carry

    lax.fori_loop(0, d_conv // LANES, lane_chunk, 0)

    for rb in range(tc // rb_rows):
        rows = pl.ds(rb * rb_rows, rb_rows)
        y = ybuf[rows, :]
        mu = jnp.mean(y, axis=-1, keepdims=True)
        yc = y - mu
        var = jnp.mean(yc * yc, axis=-1, keepdims=True)
        zn = (yc * lax.rsqrt(var + EPS)) * lg_ref[...] + lb_ref[...]
        z_ref[rows, :] = (zn * _sigmoid(zn)).astype(BF16)


def _conv(u2d, ctx, w_pad, bias, ln_g, ln_b, *, n_seq, n_taps, row_step, tc, rb_rows):
    m, d_conv = u2d.shape
    halo = ctx.shape[1]
    tiles_per_seq = (m // n_seq) // tc
    kernel = functools.partial(_conv_kernel, n_taps=n_taps, row_step=row_step, halo=halo,
                               rb_rows=rb_rows)
    return pl.pallas_call(
        kernel,
        grid=(n_seq, tiles_per_seq),
        in_specs=[
            pl.BlockSpec((tc, d_conv), lambda b, i: (b * tiles_per_seq + i, 0)),
            pl.BlockSpec((1, halo, d_conv), lambda b, i: (b, 0, 0)),
            _const_spec(w_pad.shape),
            _const_spec((1, d_conv)),
            _const_spec((1, d_conv)),
            _const_spec((1, d_conv)),
        ],
        out_specs=pl.BlockSpec((tc, d_conv), lambda b, i: (b * tiles_per_seq + i, 0)),
        out_shape=jax.ShapeDtypeStruct((m, d_conv), BF16),
        scratch_shapes=[pltpu.VMEM((halo + tc, d_conv), F32), pltpu.VMEM((tc, d_conv), F32)],
        compiler_params=_params(2),
        name="conv_branch",
    )(u2d, ctx, w_pad, bias, ln_g, ln_b)


CONV_BLOCK = 64
X_PITCH = CONV_BLOCK + 4
Y_PITCH = CONV_BLOCK + SUBLANES


def _conv_prompt_kernel(u_ref, ctx_ref, w_ref, b_ref, lg_ref, lb_ref, z_ref, xbuf, ybuf,
                        *, n_taps, r_group):
    i = pl.program_id(1)
    tc, d_conv = u_ref.shape
    n_slabs = d_conv // LANES
    n_blocks = tc // CONV_BLOCK

    @pl.when(i == 0)
    def _():
        for cc in range(n_slabs):
            xbuf[cc, 0:CONV_BLOCK, :] = ctx_ref[0, :, cc * LANES:(cc + 1) * LANES]

    @pl.when(i > 0)
    def _():
        for cc in range(n_slabs):
            xbuf[cc, 0:CONV_BLOCK, :] = xbuf[cc, pl.ds(n_blocks * X_PITCH, CONV_BLOCK), :]

    for cc in range(n_slabs):
        for blk in range(n_blocks):
            xbuf[cc, pl.ds((blk + 1) * X_PITCH, CONV_BLOCK), :] = (
                u_ref[blk * CONV_BLOCK:(blk + 1) * CONV_BLOCK, cc * LANES:(cc + 1) * LANES])

    def slab(cc, carry):
        bias = jnp.broadcast_to(b_ref[cc], (SUBLANES, LANES))
        for r0 in range(0, CONV_BLOCK, r_group):
            accs = [bias] * r_group
            for k in range(n_taps):
                wk = jnp.broadcast_to(w_ref[cc, k:k + 1, :], (SUBLANES, LANES))
                for rr in range(r_group):
                    d = r0 + rr - (n_taps - 1) + k
                    start = X_PITCH + d if d >= 0 else CONV_BLOCK + d
                    xk = xbuf[cc, pl.ds(start, SUBLANES, stride=X_PITCH), :]
                    accs[rr] = accs[rr] + wk * xk
            for rr in range(r_group):
                ybuf[cc, pl.ds(r0 + rr, SUBLANES, stride=Y_PITCH), :] = accs[rr]
        return carry

    lax.fori_loop(0, n_slabs, slab, 0)

    for blk in range(n_blocks):
        y = jnp.concatenate([ybuf[cc, pl.ds(blk * Y_PITCH, CONV_BLOCK), :]
                             for cc in range(n_slabs)], axis=1)
        mu = jnp.mean(y, axis=-1, keepdims=True)
        yc = y - mu
        var = jnp.mean(yc * yc, axis=-1, keepdims=True)
        zn = (yc * lax.rsqrt(var + EPS)) * lg_ref[...] + lb_ref[...]
        z_ref[blk * CONV_BLOCK:(blk + 1) * CONV_BLOCK, :] = (zn * _sigmoid(zn)).astype(BF16)


def _conv_prompt(u2d, ctx, w_slabs, b_slabs, ln_g, ln_b, *, n_seq, n_taps):
    m = u2d.shape[0]
    d_conv = ln_g.shape[1]
    n_slabs = d_conv // LANES
    tc = CONV_BLOCK * SUBLANES
    tiles_per_seq = (m // n_seq) // tc
    kernel = functools.partial(_conv_prompt_kernel, n_taps=n_taps, r_group=SUBLANES)
    return pl.pallas_call(
        kernel,
        grid=(n_seq, tiles_per_seq),
        in_specs=[
            pl.BlockSpec((tc, d_conv), lambda b, i: (b * tiles_per_seq + i, 0)),
            pl.BlockSpec((1, CONV_BLOCK, d_conv), lambda b, i: (b, 0, 0)),
            _const_spec(w_slabs.shape),
            _const_spec(b_slabs.shape),
            _const_spec((1, d_conv)),
            _const_spec((1, d_conv)),
        ],
        out_specs=pl.BlockSpec((tc, d_conv), lambda b, i: (b * tiles_per_seq + i, 0)),
        out_shape=jax.ShapeDtypeStruct((m, d_conv), BF16),
        scratch_shapes=[
            pltpu.VMEM((n_slabs, (SUBLANES + 1) * X_PITCH + 4, LANES), F32),
            pltpu.VMEM((n_slabs, SUBLANES * Y_PITCH, LANES), F32)],
        compiler_params=_params(2),
        name="conv_branch_prompt",
    )(u2d, ctx, w_slabs, b_slabs, ln_g, ln_b)


def _attn_kernel(q_ref, k_ref, v_ref, c_ref, o_ref, *, tq):
    h = pl.program_id(1)
    seq = q_ref.shape[1]
    nq = seq // tq
    c2 = c_ref[0, pl.ds(h, 1), :] * LOG2E
    row = lax.broadcasted_iota(jnp.int32, (tq, tq), 0)
    colk = lax.broadcasted_iota(jnp.int32, (tq, tq), 1)
    causal = colk <= row
    nt = (((1,), (1,)), ((), ()))
    qs = [q_ref[0, i * tq:(i + 1) * tq, :] for i in range(nq)]
    m = [None] * nq
    l = [None] * nq
    acc = [None] * nq
    for j in range(nq):
        keys = slice(j * tq, (j + 1) * tq)
        kj = k_ref[0, keys, :].astype(BF16)
        vj = v_ref[0, keys, :].astype(BF16)
        cj = c2[:, keys]
        for i in range(j, nq):
            s = lax.dot_general(qs[i], kj, nt, preferred_element_type=F32) - cj
            if i == j:
                s = jnp.where(causal, s, NEG_INF)
            s_max = jnp.max(s, axis=1, keepdims=True)
            if j == 0:
                m[i] = s_max
                p = jnp.exp2(s - s_max)
                l[i] = jnp.sum(p, axis=1, keepdims=True)
                acc[i] = jnp.dot(p.astype(BF16), vj, preferred_element_type=F32)
            else:
                m_new = jnp.maximum(m[i], s_max)
                alpha = jnp.exp2(m[i] - m_new)
                p = jnp.exp2(s - m_new)
                l[i] = alpha * l[i] + jnp.sum(p, axis=1, keepdims=True)
                acc[i] = alpha * acc[i] + jnp.dot(p.astype(BF16), vj,
                                                  preferred_element_type=F32)
                m[i] = m_new
        o_ref[0, keys, :] = (acc[j] / l[j]).astype(BF16)


def _attention(pb3d, k3d, v3d, c, *, q_col, tq):
    n_seq, n_heads, seq = c.shape

    def head_spec(col):
        first = col // LANES
        return pl.BlockSpec((1, seq, LANES), lambda b, h: (b, 0, first + h))

    kernel = functools.partial(_attn_kernel, tq=tq)
    return pl.pallas_call(
        kernel,
        grid=(n_seq, n_heads),
        in_specs=[head_spec(q_col), head_spec(0), head_spec(0),
                  pl.BlockSpec((1, n_heads, seq), lambda b, h: (b, 0, 0))],
        out_specs=head_spec(0),
        out_shape=jax.ShapeDtypeStruct((n_seq, seq, n_heads * LANES), BF16),
        compiler_params=_params(2),
        name="fox_attention_prompt",
    )(pb3d, k3d, v3d, c)


def _attn_sample_kernel(q_ref, kp_ref, vp_ref, kn_ref, vn_ref, c_ref, o_ref, *, past, n_heads):
    t = q_ref.shape[0]
    nt = (((1,), (1,)), ((), ()))
    row = lax.broadcasted_iota(jnp.int32, (t, t), 0)
    colk = lax.broadcasted_iota(jnp.int32, (t, t), 1)
    causal = colk <= row
    for h in range(n_heads):
        lanes = slice(h * LANES, (h + 1) * LANES)
        q = q_ref[:, lanes]
        c2 = c_ref[0, h:h + 1, :] * LOG2E
        k_past = kp_ref[0, pl.ds(h, past, stride=n_heads), :].astype(BF16)
        v_past = vp_ref[0, pl.ds(h, past, stride=n_heads), :].astype(BF16)
        s_p = lax.dot_general(q, k_past, nt, preferred_element_type=F32) - c2[:, :past]
        s_n = lax.dot_general(q, kn_ref[:, lanes].astype(BF16), nt,
                              preferred_element_type=F32) - c2[:, past:past + t]
        s_n = jnp.where(causal, s_n, NEG_INF)
        m = jnp.maximum(jnp.max(s_p, axis=1, keepdims=True), jnp.max(s_n, axis=1, keepdims=True))
        p_p = jnp.exp2(s_p - m)
        p_n = jnp.exp2(s_n - m)
        l = jnp.sum(p_p, axis=1, keepdims=True) + jnp.sum(p_n, axis=1, keepdims=True)
        acc = jnp.dot(p_p.astype(BF16), v_past, preferred_element_type=F32)
        acc = acc + jnp.dot(p_n.astype(BF16), vn_ref[:, lanes].astype(BF16),
                            preferred_element_type=F32)
        o_ref[:, lanes] = (acc / l).astype(BF16)


def _attention_sample(pb2d, k_new, v_new, k_past, v_past, c, *, q_col, t):
    n_seq, past_rows, _ = k_past.shape
    n_heads = c.shape[1]
    att_w = n_heads * LANES
    kernel = functools.partial(_attn_sample_kernel, past=past_rows // n_heads, n_heads=n_heads)

    def new_spec(col):
        blk = col // att_w
        return pl.BlockSpec((t, att_w), lambda b: (b, blk))

    past_spec = pl.BlockSpec((1, past_rows, LANES), lambda b: (b, 0, 0))
    return pl.pallas_call(
        kernel,
        grid=(n_seq,),
        in_specs=[new_spec(q_col), past_spec, past_spec, new_spec(0), new_spec(0),
                  pl.BlockSpec((1, n_heads, c.shape[2]), lambda b: (b, 0, 0))],
        out_specs=new_spec(0),
        out_shape=jax.ShapeDtypeStruct((n_seq * t, att_w), BF16),
        compiler_params=_params(1),
        name="fox_attention_sample",
    )(pb2d, k_past, v_past, k_new, v_new, c)


def _mix_kernel(z_ref, o_ref, ga_ref, gb_ref, x_ref, pw_ref, wa_ref, wo_ref, y_ref):
    a_out = jnp.dot(z_ref[...], pw_ref[...], preferred_element_type=F32)
    b_out = jnp.dot(o_ref[...], wa_ref[...], preferred_element_type=F32)
    mix = ga_ref[...].astype(F32) * a_out + gb_ref[...].astype(F32) * b_out
    y_ref[...] = x_ref[...] + jnp.dot(mix.astype(BF16), wo_ref[...], preferred_element_type=F32)


def _mix(z, o, gates, x2d, pw, wa, wo, *, tm):
    m, d_model = x2d.shape

    def row(width, blk=0):
        return pl.BlockSpec((tm, width), lambda i: (i, blk))

    return pl.pallas_call(
        _mix_kernel,
        grid=(m // tm,),
        in_specs=[row(z.shape[1]), row(o.shape[1]), row(d_model, 0), row(d_model, 1),
                  row(d_model), _const_spec(pw.shape), _const_spec(wa.shape),
                  _const_spec(wo.shape)],
        out_specs=row(d_model),
        out_shape=jax.ShapeDtypeStruct((m, d_model), F32),
        compiler_params=_params(1),
        name="mix_out",
    )(z, o, gates, gates, x2d, pw, wa, wo)


def _ffn_kernel(x_ref, g_ref, wup_hbm, wd_hbm, cw_ref, cb_ref, st_ref,
                y_ref, nf_ref, h_ref, gbuf, carry, wg_buf, wu_buf, wd_buf, sem,
                *, n_row_tiles, tiles_per_seq, row_step, halo, rc, rc_pre, tf):
    i = pl.program_id(0)
    tm = x_ref.shape[0]
    d_ff = wd_hbm.shape[0]
    n_tiles = d_ff // tf
    n_chunks = tm // rc
    n_total = n_row_tiles * n_tiles

    def weight_copies(t, slot):
        c0 = pl.multiple_of(t * tf, tf)
        return (
            pltpu.make_async_copy(wup_hbm.at[:, pl.ds(c0, tf)], wg_buf.at[slot], sem.at[0, slot]),
            pltpu.make_async_copy(wup_hbm.at[:, pl.ds(d_ff + c0, tf)], wu_buf.at[slot],
                                  sem.at[1, slot]),
            pltpu.make_async_copy(wd_hbm.at[pl.ds(c0, tf), :], wd_buf.at[slot], sem.at[2, slot]),
        )

    @pl.when(i == 0)
    def _():
        for cp in weight_copies(0, 0):
            cp.start()

    for c in range(tm // rc_pre):
        rows = pl.ds(c * rc_pre, rc_pre)
        x = x_ref[rows, :]
        ms = jnp.mean(x * x, axis=-1, keepdims=True)
        h_ref[rows, :] = ((x * lax.rsqrt(ms + EPS)) * g_ref[...]).astype(BF16)
        y_ref[rows, :] = x

    first_in_seq = (i % tiles_per_seq) == 0

    def column_tile(t, loop_carry):
        it = i * n_tiles + t
        slot = it % 2
        for cp in weight_copies(t, slot):
            cp.wait()

        @pl.when(it + 1 < n_total)
        def _():
            t_next = jnp.where(t + 1 == n_tiles, 0, t + 1)
            for cp in weight_copies(t_next, 1 - slot):
                cp.start()

        cols = pl.ds(pl.multiple_of(t * tf, tf), tf)

        @pl.when(first_in_seq)
        def _():
            gbuf[0:halo, :] = st_ref[0, :, cols]

        @pl.when(jnp.logical_not(first_in_seq))
        def _():
            gbuf[0:halo, :] = carry[:, cols]

        w0 = cw_ref[0:1, cols]
        w1 = cw_ref[1:2, cols]
        w2 = cw_ref[2:3, cols]
        bias = cb_ref[:, cols]
        for c in range(n_chunks):
            rows = pl.ds(c * rc, rc)
            hb = h_ref[rows, :]
            g = jnp.dot(hb, wg_buf[slot], preferred_element_type=F32)
            up = jnp.dot(hb, wu_buf[slot], preferred_element_type=F32)
            gbuf[pl.ds(halo + c * rc, rc), :] = g
            g2 = gbuf[pl.ds(halo + c * rc - 2 * row_step, rc), :]
            g1 = gbuf[pl.ds(halo + c * rc - row_step, rc), :]
            gc = w0 * g2 + w1 * g1 + w2 * g + bias
            act = ((gc * _sigmoid(gc)) * up).astype(BF16)
            y_ref[rows, :] += jnp.dot(act, wd_buf[slot], preferred_element_type=F32)

        tail = gbuf[tm:tm + halo, :]
        carry[:, cols] = tail
        nf_ref[0, :, cols] = tail
        return loop_carry

    lax.fori_loop(0, n_tiles, column_tile, 0)


def _ffn(x2d, norm_g, w_up, wd, conv_w, conv_b, state, *, n_seq, row_step, tm, tf, rc):
    m, d_model = x2d.shape
    d_ff = wd.shape[0]
    n_ff_tiles = d_ff // tf
    halo = state.shape[1]
    n_row_tiles = m // tm
    tiles_per_seq = (m // n_seq) // tm
    kernel = functools.partial(_ffn_kernel, n_row_tiles=n_row_tiles, tiles_per_seq=tiles_per_seq,
                               row_step=row_step, halo=halo, rc=rc, rc_pre=min(256, tm), tf=tf)
    y, new_state = pl.pallas_call(
        kernel,
        grid=(n_row_tiles,),
        in_specs=[
            pl.BlockSpec((tm, d_model), lambda i: (i, 0)),
            _const_spec((1, d_model)),
            pl.BlockSpec(memory_space=pl.ANY),
            pl.BlockSpec(memory_space=pl.ANY),
            _const_spec((SUBLANES, d_ff)),
            _const_spec((1, d_ff)),
            pl.BlockSpec((1, halo, d_ff), lambda i: (i // tiles_per_seq, 0, 0)),
        ],
        out_specs=(
            pl.BlockSpec((tm, d_model), lambda i: (i, 0)),
            pl.BlockSpec((1, halo, d_ff), lambda i: (i, 0, 0)),
        ),
        out_shape=(jax.ShapeDtypeStruct((m, d_model), F32),
                   jax.ShapeDtypeStruct((n_row_tiles, halo, d_ff), F32)),
        scratch_shapes=[pltpu.VMEM((tm, d_model), BF16),
                        pltpu.VMEM((halo + tm, tf), F32),
                        pltpu.VMEM((halo, d_ff), F32),
                        pltpu.VMEM((2, d_model, tf), BF16),
                        pltpu.VMEM((2, d_model, tf), BF16),
                        pltpu.VMEM((2, tf, d_model), BF16),
                        pltpu.SemaphoreType.DMA((3, 2))],
        compiler_params=_params(1),
        name="conv_ffn",
    )(x2d, norm_g, w_up, wd, conv_w, conv_b, state)
    return y, new_state[tiles_per_seq - 1::tiles_per_seq]


def _prep_weights(norm_mix_g, w_in, b_forget, conv_dw_w, conv_dw_b, conv_ln_g, conv_ln_b,
                  conv_pw_out, q_norm_g, k_norm_g, w_att_out, w_out, norm_ffn_g, w_up,
                  ffn_dw_w, ffn_dw_b, w_down, *, n_heads):
    d_model = w_in.shape[0]
    d_conv = conv_dw_w.shape[1]
    att_w = w_att_out.shape[0]
    d_ff = w_down.shape[0]
    o0 = 2 * d_conv
    o3 = o0 + 3 * att_w
    o4 = o3 + n_heads
    n_glu = d_conv // GLU_HALF
    glu_cols = []
    for t in range(n_glu):
        glu_cols.append(w_in[:, t * GLU_HALF:(t + 1) * GLU_HALF])
        glu_cols.append(w_in[:, d_conv + t * GLU_HALF:d_conv + (t + 1) * GLU_HALF])
    w_tiles = jnp.concatenate(glu_cols + [w_in[:, o0:o3], w_in[:, o4:]], axis=1).astype(BF16)
    wf_pad = jnp.pad(w_in[:, o3:o4], ((0, 0), (0, LANES - n_heads))).astype(BF16)
    bf_pad = jnp.pad(b_forget.reshape(1, n_heads), ((0, 0), (0, LANES - n_heads)))
    n_taps = conv_dw_w.shape[0]
    conv_w = jnp.pad(conv_dw_w, ((0, -n_taps % SUBLANES), (0, 0)))
    n_slabs = d_conv // LANES
    return dict(
        norm_mix_g=norm_mix_g.reshape(1, d_model), w_tiles=w_tiles, wf_pad=wf_pad, bf_pad=bf_pad,
        q_gain=q_norm_g.reshape(1, LANES), k_gain=k_norm_g.reshape(1, LANES),
        conv_w=conv_w,
        conv_w_slabs=conv_w.reshape(-1, n_slabs, LANES).transpose(1, 0, 2),
        conv_b=conv_dw_b.reshape(1, d_conv), conv_b_slabs=conv_dw_b.reshape(n_slabs, 1, LANES),
        ln_g=conv_ln_g.reshape(1, d_conv), ln_b=conv_ln_b.reshape(1, d_conv),
        pw=conv_pw_out.astype(BF16), wa=w_att_out.astype(BF16), wo=w_out.astype(BF16),
        norm_ffn_g=norm_ffn_g.reshape(1, d_model),
        w_up=w_up.astype(BF16), wd=w_down.astype(BF16),
        ffn_w=jnp.pad(ffn_dw_w, ((0, SUBLANES - ffn_dw_w.shape[0]), (0, 0))),
        ffn_b=ffn_dw_b.reshape(1, d_ff), n_taps=n_taps, n_ffn_taps=ffn_dw_w.shape[0],
        d_conv=d_conv, att_w=att_w, d_ff=d_ff,
    )


def _time_major(a, n_seq, t):
    return a.reshape(n_seq, t, -1).transpose(1, 0, 2).reshape(n_seq * t, -1)


def _seq_major(a, n_seq, t):
    return a.reshape(t, n_seq, -1).transpose(1, 0, 2).reshape(n_seq * t, -1)


def _layer_prompt(x, w, *, n_heads):
    n_seq, seq, d_model = x.shape
    m = n_seq * seq
    d_conv, att_w, d_ff = w["d_conv"], w["att_w"], w["d_ff"]
    x2d = x.reshape(m, d_model)
    tm = min(1024, seq)
    u, kf, vf, pb, lf, lft = _in_proj(
        x2d, w["norm_mix_g"], w["w_tiles"], w["wf_pad"], w["bf_pad"], w["q_gain"], w["k_gain"],
        n_seq=n_seq, d_conv=d_conv, att_w=att_w, n_heads=n_heads, tm=tm, rc=min(256, tm))
    q_col = 2 * d_model
    c = _cumsum(lft)
    ctx = jnp.zeros((n_seq, CONV_BLOCK, d_conv), F32)
    z = _conv_prompt(u, ctx, w["conv_w_slabs"], w["conv_b_slabs"], w["ln_g"], w["ln_b"],
                     n_seq=n_seq, n_taps=w["n_taps"])
    o = _attention(pb.reshape(n_seq, seq, -1), kf.reshape(n_seq, seq, att_w),
                   vf.reshape(n_seq, seq, att_w), c, q_col=q_col,
                   tq=min(512, seq)).reshape(m, att_w)
    x1 = _mix(z, o, pb, x2d, w["pw"], w["wa"], w["wo"], tm=min(512, seq))
    state = jnp.zeros((n_seq, SUBLANES, d_ff), F32)
    y, new_state = _ffn(x1, w["norm_ffn_g"], w["w_up"], w["wd"], w["ffn_w"], w["ffn_b"],
                        state, n_seq=n_seq, row_step=1, tm=tm, tf=FFN_TILE, rc=tm)
    n_keep = w["n_taps"] - 1
    new_conv = u.reshape(n_seq, seq, d_conv)[:, seq - n_keep:]
    new_ffn = new_state[:, SUBLANES - (w["n_ffn_taps"] - 1):]
    return (y.reshape(n_seq, seq, d_model), new_conv, new_ffn,
            kf.reshape(n_seq, seq, n_heads, LANES), vf.reshape(n_seq, seq, n_heads, LANES),
            lf.reshape(n_seq, seq, n_heads))


def _layer_sample(x, conv_buf, ffn_buf, past_k, past_v, past_logf, w, *, n_heads):
    n_seq, t, d_model = x.shape
    m = n_seq * t
    past = past_k.shape[1]
    d_conv, att_w, d_ff = w["d_conv"], w["att_w"], w["d_ff"]
    x2d = x.reshape(m, d_model)
    u, kf, vf, pb, lf, lft = _in_proj(
        x2d, w["norm_mix_g"], w["w_tiles"], w["wf_pad"], w["bf_pad"], w["q_gain"], w["k_gain"],
        n_seq=1, d_conv=d_conv, att_w=att_w, n_heads=n_heads, tm=m, rc=m)
    q_col = 2 * d_model
    lft_new = lft.reshape(n_heads, n_seq, t).transpose(1, 0, 2)
    pad = -(past + t) % LANES
    lf_all = jnp.concatenate([past_logf.astype(F32).transpose(0, 2, 1), lft_new,
                              jnp.zeros((n_seq, n_heads, pad), F32)], axis=2)
    c = _cumsum(lf_all)
    n_keep = w["n_taps"] - 1
    ctx = conv_buf.astype(F32).transpose(1, 0, 2).reshape(1, n_keep * n_seq, d_conv)
    z_tm = _conv(_time_major(u, n_seq, t), ctx, w["conv_w"], w["conv_b"], w["ln_g"], w["ln_b"],
                 n_seq=1, n_taps=w["n_taps"], row_step=n_seq, tc=m, rb_rows=64)
    z = _seq_major(z_tm, n_seq, t)
    o = _attention_sample(pb, kf, vf, past_k.reshape(n_seq, past * n_heads, LANES),
                          past_v.reshape(n_seq, past * n_heads, LANES), c, q_col=q_col, t=t)
    x1 = _mix(z, o, pb, x2d, w["pw"], w["wa"], w["wo"], tm=m)
    n_fk = w["n_ffn_taps"] - 1
    state = ffn_buf.astype(F32).transpose(1, 0, 2).reshape(1, n_fk * n_seq, d_ff)
    y_tm, new_state = _ffn(_time_major(x1, n_seq, t), w["norm_ffn_g"], w["w_up"], w["wd"],
                           w["ffn_w"], w["ffn_b"], state, n_seq=1, row_step=n_seq, tm=m,
                           tf=FFN_TILE, rc=m)
    y = _seq_major(y_tm, n_seq, t)
    new_conv = jnp.concatenate([conv_buf.astype(F32), u.reshape(n_seq, t, d_conv)],
                               axis=1)[:, -n_keep:]
    new_ffn = new_state.reshape(n_fk, n_seq, d_ff).transpose(1, 0, 2)
    return (y.reshape(n_seq, t, d_model), new_conv, new_ffn,
            kf.reshape(n_seq, t, n_heads, LANES), vf.reshape(n_seq, t, n_heads, LANES),
            lf.reshape(n_seq, t, n_heads))


def kernel(x_prompt, x_sample, cache_k, cache_v, cache_logf, state_conv, state_ffn, norm_mix_g, w_in, b_forget, conv_dw_w, conv_dw_b, conv_ln_g, conv_ln_b, conv_pw_out, q_norm_g, k_norm_g, w_att_out, w_out, norm_ffn_g, w_up, ffn_dw_w, ffn_dw_b, w_down):
    depth = w_in.shape[0]
    n_heads = cache_k.shape[3]
    assert cache_k.shape[4] == LANES
    xp, xs = x_prompt, x_sample
    outs_p, outs_s = [], []
    for l in range(depth):
        w = _prep_weights(norm_mix_g[l], w_in[l], b_forget[l], conv_dw_w[l], conv_dw_b[l],
                          conv_ln_g[l], conv_ln_b[l], conv_pw_out[l], q_norm_g[l], k_norm_g[l],
                          w_att_out[l], w_out[l], norm_ffn_g[l], w_up[l], ffn_dw_w[l],
                          ffn_dw_b[l], w_down[l], n_heads=n_heads)
        xs, *rest_s = _layer_sample(xs, state_conv[l], state_ffn[l], cache_k[l], cache_v[l],
                                    cache_logf[l], w, n_heads=n_heads)
        xp, xs = lax.optimization_barrier((xp, xs))
        xp, *rest_p = _layer_prompt(xp, w, n_heads=n_heads)
        outs_p.append(rest_p)
        outs_s.append(rest_s)

    def stack(outs, idx):
        return jnp.stack([o[idx] for o in outs])

    return (xp, xs,
            stack(outs_p, 2), stack(outs_p, 3), stack(outs_p, 4), stack(outs_p, 0), stack(outs_p, 1),
            stack(outs_s, 2), stack(outs_s, 3), stack(outs_s, 4), stack(outs_s, 0), stack(outs_s, 1))
```
